```python
import math
import jax, jax.numpy as jnp
from jax import lax
import numpy as np

D_MODEL = 1024
BATCH = 2
SEQ = 8192
DEPTH = 4

DA_HEADS = 4
DA_HEAD_DIM = 64
DA_V_DIM = 2 * DA_HEAD_DIM
DA_WIDTH = DA_HEADS * DA_V_DIM
Q_BLOCK = 128
ML_HEADS = 4
ML_HEAD_DIM = 128
ML_WIDTH = ML_HEADS * ML_HEAD_DIM
ML_CONV = 5
ML_CHUNK = 128
WA_Q_HEADS = 8
WA_KV_HEADS = 2
WA_HEAD_DIM = 64
WA_WIDTH = WA_Q_HEADS * WA_HEAD_DIM
WINDOW = 128
WA_BLOCK = 128
ROPE_THETA = 10000.0
N_BRANCH = 3
N_EXPERTS = 32
TOP_K = 4
D_FF = D_MODEL
SWIGLU_LIMIT = 7.0
SWIGLU_ALPHA = 1.702
DN_ALPHA = (2.0 * DEPTH) ** 0.25
DN_BETA = (8.0 * DEPTH) ** -0.25
EPS = 1e-5

DA_Q = 0
DA_K = DA_Q + DA_HEADS * 2 * DA_HEAD_DIM
DA_V = DA_K + DA_HEADS * 2 * DA_HEAD_DIM
ML_X = DA_V + DA_WIDTH
ML_O = ML_X + ML_WIDTH
WA_Q = ML_O + ML_WIDTH
WA_K = WA_Q + WA_WIDTH
WA_V = WA_K + WA_KV_HEADS * WA_HEAD_DIM
GATES = WA_V + WA_KV_HEADS * WA_HEAD_DIM
D_IN = GATES + N_BRANCH * D_MODEL

kernel_name = 'hybrid_diffattn_mlstm_swa_moe_encoder'


def layer_norm(x, g, b):
    xf = x.astype(jnp.float32)
    mu = jnp.mean(xf, axis=-1, keepdims=True)
    var = jnp.mean(jnp.square(xf - mu), axis=-1, keepdims=True)
    return ((xf - mu) * lax.rsqrt(var + EPS) * g + b).astype(x.dtype)


def rms_norm(x, w):
    xf = x.astype(jnp.float32)
    return xf * lax.rsqrt(jnp.mean(jnp.square(xf), axis=-1, keepdims=True) + EPS) * w


def rope_tables(seq, dim):
    pos = jnp.arange(seq, dtype=jnp.float32)
    inv = ROPE_THETA ** (-jnp.arange(0, dim, 2, dtype=jnp.float32) / dim)
    ang = pos[:, None] * inv[None, :]
    return jnp.cos(ang), jnp.sin(ang)


def apply_rope(x, cos, sin):
    half = x.shape[-1] // 2
    shp = (cos.shape[0],) + (1,) * (x.ndim - 3) + (half,)
    c, s = cos.reshape(shp), sin.reshape(shp)
    xf = x.astype(jnp.float32)
    x1, x2 = xf[..., :half], xf[..., half:]
    return jnp.concatenate([x1 * c - x2 * s, x2 * c + x1 * s], axis=-1)


def diff_attention(q, k, v, lam, subln_w, lam_init):
    B, S, H, _, dh = q.shape
    nb = S // Q_BLOCK
    qb = (q * dh ** -0.5).reshape(B, nb, Q_BLOCK, H, 2, dh).transpose(1, 0, 3, 4, 2, 5)
    kt = k.astype(jnp.float32).transpose(0, 2, 3, 1, 4)
    vt = v.astype(jnp.float32).transpose(0, 2, 1, 3)

    def block(qi):
        s = jnp.einsum('bhmqd,bhmkd->bhmqk', qi, kt)
        p = jax.nn.softmax(s, axis=-1)
        a = p[:, :, 0] - lam * p[:, :, 1]
        return jnp.einsum('bhqk,bhkd->bhqd', a, vt)

    o = lax.map(block, qb)
    o = o.transpose(1, 0, 3, 2, 4).reshape(B, S, H, DA_V_DIM)
    o = rms_norm(o, subln_w) * (1.0 - lam_init)
    return o.reshape(B, S, H * DA_V_DIM)


def window_attention(q, k, v, sinks):
    B, S, Hq, d = q.shape
    Hkv = k.shape[2]
    R = Hq // Hkv
    nb = S // WA_BLOCK
    qb = (q * d ** -0.5).reshape(B, nb, WA_BLOCK, Hkv, R, d)

    def band(t):
        tp = jnp.pad(t.astype(jnp.float32), ((0, 0), (WINDOW, WINDOW), (0, 0), (0, 0)))
        tp = tp.reshape(B, nb + 2, WA_BLOCK, Hkv, d)
        return jnp.concatenate([tp[:, :-2], tp[:, 1:-1], tp[:, 2:]], axis=2)

    kw, vw = band(k), band(v)
    s = jnp.einsum('bnqgrd,bnkgd->bngrqk', qb, kw)
    qpos = jnp.arange(S).reshape(nb, WA_BLOCK)
    kpos = jnp.arange(nb)[:, None] * WA_BLOCK - WINDOW + jnp.arange(3 * WA_BLOCK)[None, :]
    valid = ((jnp.abs(qpos[:, :, None] - kpos[:, None, :]) <= WINDOW)
             & (kpos >= 0)[:, None, :] & (kpos < S)[:, None, :])
    s = jnp.where(valid[None, :, None, None], s, -jnp.inf)
    sink = sinks.astype(jnp.float32).reshape(Hkv, R)[None, None, :, :, None, None]
    m = jnp.maximum(jnp.max(s, axis=-1, keepdims=True), sink)
    e = jnp.exp(s - m)
    p = e / (jnp.sum(e, axis=-1, keepdims=True) + jnp.exp(sink - m))
    o = jnp.einsum('bngrqk,bnkgd->bnqgrd', p, vw)
    return o.reshape(B, S, Hq * d)


def mlstm_chunkwise(q, k, v, i_pre, f_pre):
    B, H, S, d = q.shape
    L = ML_CHUNK
    nc = S // L
    q = q.reshape(B, H, nc, L, d)
    k = k.reshape(B, H, nc, L, d) * d ** -0.5
    v = v.reshape(B, H, nc, L, d)
    ig = i_pre.reshape(B, H, nc, L)
    b = jnp.cumsum(jax.nn.log_sigmoid(f_pre).reshape(B, H, nc, L), axis=-1)
    g = b[..., -1]
    w_end = g[..., None] - b + ig
    m_loc = jnp.max(w_end, axis=-1)
    e_end = jnp.exp(w_end - m_loc[..., None])
    C_loc = jnp.einsum('bhcs,bhcsk,bhcsv->bhckv', e_end, k, v)
    n_loc = jnp.einsum('bhcs,bhcsk->bhck', e_end, k)

    def step(carry, inp):
        C, n, m = carry
        g_c, m_l, C_l, n_l = inp
        m_new = jnp.maximum(g_c + m, m_l)
        a = jnp.exp(g_c + m - m_new)
        bl = jnp.exp(m_l - m_new)
        C_new = a[..., None, None] * C + bl[..., None, None] * C_l
        n_new = a[..., None] * n + bl[..., None] * n_l
        return (C_new, n_new, m_new), (C, n, m)

    init = (jnp.zeros((B, H, d, d), jnp.float32), jnp.zeros((B, H, d), jnp.float32),
            jnp.zeros((B, H), jnp.float32))
    xs = (jnp.moveaxis(g, 2, 0), jnp.moveaxis(m_loc, 2, 0),
          jnp.moveaxis(C_loc, 2, 0), jnp.moveaxis(n_loc, 2, 0))
    _, (C_prev, n_prev, m_prev) = lax.scan(step, init, xs)
    C_prev = jnp.moveaxis(C_prev, 0, 2)
    n_prev = jnp.moveaxis(n_prev, 0, 2)
    m_prev = jnp.moveaxis(m_prev, 0, 2)
    D = b[..., :, None] - b[..., None, :] + ig[..., None, :]
    tril = jnp.tril(jnp.ones((L, L), dtype=bool))
    D = jnp.where(tril, D, -jnp.inf)
    a_t = b + m_prev[..., None]
    m_t = jnp.maximum(a_t, jnp.max(D, axis=-1))
    W = jnp.exp(D - m_t[..., None]) * jnp.einsum('bhctd,bhcsd->bhcts', q, k)
    inter = jnp.exp(a_t - m_t)
    numer = (inter[..., None] * jnp.einsum('bhctk,bhckv->bhctv', q, C_prev)
             + jnp.einsum('bhcts,bhcsv->bhctv', W, v))
    denom = inter * jnp.einsum('bhctk,bhck->bhct', q, n_prev) + jnp.sum(W, axis=-1)
    h = numer / jnp.maximum(jnp.abs(denom), jnp.exp(-m_t))[..., None]
    return h.reshape(B, H, S, d)


def mlstm_branch(xm, o_pre, conv_w, conv_b, wq, wk, wv, w_g, b_g, norm_w, skip):
    B, S, W = xm.shape
    H, d = ML_HEADS, ML_HEAD_DIM
    xc = lax.conv_general_dilated(xm, conv_w[:, None, :], window_strides=(1,),
                                  padding=[(ML_CONV // 2, ML_CONV // 2)],
                                  dimension_numbers=('NWC', 'WIO', 'NWC'),
                                  feature_group_count=W) + conv_b
    xc = jax.nn.silu(xc)
    xch = xc.reshape(B, S, H, d)
    q = jnp.einsum('bshd,hde->bshe', xch, wq)
    k = jnp.einsum('bshd,hde->bshe', xch, wk)
    v = jnp.einsum('bshd,hde->bshe', xm.reshape(B, S, H, d), wv)
    gin = jnp.concatenate([q.reshape(B, S, W), k.reshape(B, S, W), v.reshape(B, S, W)], axis=-1)
    gts = (gin @ w_g + b_g).astype(jnp.float32).reshape(B, S, 4, H).transpose(2, 0, 3, 1)
    i_f, f_f, i_b, f_b = gts[0], gts[1], gts[2], gts[3]
    qf = q.astype(jnp.float32).transpose(0, 2, 1, 3)
    kf = k.astype(jnp.float32).transpose(0, 2, 1, 3)
    vf = v.astype(jnp.float32).transpose(0, 2, 1, 3)
    flip = lambda t: jnp.flip(t, axis=2)
    h_fwd = mlstm_chunkwise(qf, kf, vf, i_f, f_f)
    h_bwd = flip(mlstm_chunkwise(flip(qf), flip(kf), flip(vf), flip(i_b), flip(f_b)))
    h = (h_fwd + h_bwd).transpose(0, 2, 1, 3)
    mu = jnp.mean(h, axis=-1, keepdims=True)
    var = jnp.mean(jnp.square(h - mu), axis=-1, keepdims=True)
    h = (h - mu) * lax.rsqrt(var + EPS) * norm_w.reshape(H, d)
    h = h.reshape(B, S, W) + skip * xc
    return jax.nn.sigmoid(o_pre) * h


def moe(x, router_w, router_b, w_gu, b_gu, w_down, b_down):
    B, S, Dm = x.shape
    t = x.reshape(B * S, Dm)
    logits = (t @ router_w + router_b).astype(jnp.float32)
    top_v, top_i = lax.top_k(logits, TOP_K)
    top_w = jax.nn.softmax(top_v, axis=-1)
    combine = jnp.sum(jax.nn.one_hot(top_i, N_EXPERTS, dtype=jnp.float32) * top_w[..., None], axis=1)
    y = jnp.zeros((B * S, Dm), jnp.float32)
    for e in range(N_EXPERTS):
        h = t @ w_gu[e] + b_gu[e]
        gate = jnp.minimum(h[:, ::2], SWIGLU_LIMIT)
        up = jnp.clip(h[:, 1::2], -SWIGLU_LIMIT, SWIGLU_LIMIT)
        act = gate * jax.nn.sigmoid(SWIGLU_ALPHA * gate) * (up + 1.0)
        y = y + combine[:, e:e + 1] * (act @ w_down[e] + b_down[e])
    return y.reshape(B, S, Dm).astype(x.dtype)


def setup_inputs(seed: int = 0) -> dict:
    key = jax.random.key(seed)
    ks = jax.random.split(key, 32)
    f32 = jnp.float32
    nrm = lambda k, shp, sc: jax.random.normal(k, shp, f32) * sc
    bi = nrm(ks[11], (DEPTH, 2, ML_HEADS), 0.1)
    bf = jnp.linspace(3.0, 6.0, ML_HEADS, dtype=f32)[None, None, :] + nrm(ks[12], (DEPTH, 2, ML_HEADS), 0.1)
    ml_b_gates = jnp.stack([bi[:, 0], bf[:, 0], bi[:, 1], bf[:, 1]], axis=1).reshape(DEPTH, 4 * ML_HEADS)
    return {
        'x': nrm(ks[0], (BATCH, SEQ, D_MODEL), 1.0),
        'w_in': nrm(ks[1], (DEPTH, D_MODEL, D_IN), D_MODEL ** -0.5),
        'da_lambda': nrm(ks[2], (DEPTH, 4, DA_HEAD_DIM), 0.1),
        'da_subln': 1.0 + nrm(ks[3], (DEPTH, DA_V_DIM), 0.02),
        'w_proj_a': nrm(ks[4], (DEPTH, DA_WIDTH, D_MODEL), DA_WIDTH ** -0.5),
        'ml_conv_w': nrm(ks[5], (DEPTH, ML_CONV, ML_WIDTH), ML_CONV ** -0.5),
        'ml_conv_b': nrm(ks[6], (DEPTH, ML_WIDTH), 0.02),
        'ml_wq': nrm(ks[7], (DEPTH, ML_HEADS, ML_HEAD_DIM, ML_HEAD_DIM), ML_HEAD_DIM ** -0.5),
        'ml_wk': nrm(ks[8], (DEPTH, ML_HEADS, ML_HEAD_DIM, ML_HEAD_DIM), ML_HEAD_DIM ** -0.5),
        'ml_wv': nrm(ks[9], (DEPTH, ML_HEADS, ML_HEAD_DIM, ML_HEAD_DIM), ML_HEAD_DIM ** -0.5),
        'ml_w_gates': nrm(ks[10], (DEPTH, 3 * ML_WIDTH, 4 * ML_HEADS), 0.01),
        'ml_b_gates': ml_b_gates,
        'ml_norm': 1.0 + nrm(ks[13], (DEPTH, ML_WIDTH), 0.02),
        'ml_skip': 1.0 + nrm(ks[14], (DEPTH, ML_WIDTH), 0.02),
        'w_proj_b': nrm(ks[15], (DEPTH, ML_WIDTH, D_MODEL), ML_WIDTH ** -0.5),
        'wa_sinks': nrm(ks[16], (DEPTH, WA_Q_HEADS), 0.5),
        'w_proj_c': nrm(ks[17], (DEPTH, WA_WIDTH, D_MODEL), WA_WIDTH ** -0.5),
        'w_out': nrm(ks[18], (DEPTH, D_MODEL, D_MODEL), D_MODEL ** -0.5 * DN_BETA),
        'ln1_g': 1.0 + nrm(ks[19], (DEPTH, D_MODEL), 0.02),
        'ln1_b': nrm(ks[20], (DEPTH, D_MODEL), 0.02),
        'router_w': nrm(ks[21], (DEPTH, D_MODEL, N_EXPERTS), D_MODEL ** -0.5),
        'router_b': nrm(ks[22], (DEPTH, N_EXPERTS), 0.01),
        'exp_w_gu': nrm(ks[23], (DEPTH, N_EXPERTS, D_MODEL, 2 * D_FF), D_MODEL ** -0.5),
        'exp_b_gu': nrm(ks[24], (DEPTH, N_EXPERTS, 2 * D_FF), 0.02),
        'exp_w_down': nrm(ks[25], (DEPTH, N_EXPERTS, D_FF, D_MODEL), D_FF ** -0.5 * DN_BETA),
        'exp_b_down': nrm(ks[26], (DEPTH, N_EXPERTS, D_MODEL), 0.02),
        'ln2_g': 1.0 + nrm(ks[27], (DEPTH, D_MODEL), 0.02),
        'ln2_b': nrm(ks[28], (DEPTH, D_MODEL), 0.02),
    }


def reference(x, w_in, da_lambda, da_subln, w_proj_a, ml_conv_w, ml_conv_b, ml_wq, ml_wk, ml_wv,
              ml_w_gates, ml_b_gates, ml_norm, ml_skip, w_proj_b, wa_sinks, w_proj_c, w_out,
              ln1_g, ln1_b, router_w, router_b, exp_w_gu, exp_b_gu, exp_w_down, exp_b_down,
              ln2_g, ln2_b):
    B, S, _ = x.shape
    cos, sin = rope_tables(S, DA_HEAD_DIM)
    for l in range(DEPTH):
        p = x @ w_in[l]
        qa = apply_rope(p[..., DA_Q:DA_K].reshape(B, S, DA_HEADS, 2, DA_HEAD_DIM), cos, sin)
        ka = apply_rope(p[..., DA_K:DA_V].reshape(B, S, DA_HEADS, 2, DA_HEAD_DIM), cos, sin)
        va = p[..., DA_V:ML_X].reshape(B, S, DA_HEADS, DA_V_DIM)
        lam_init = 0.8 - 0.6 * math.exp(-0.3 * l)
        lp = da_lambda[l].astype(jnp.float32)
        lam = jnp.exp(jnp.sum(lp[0] * lp[1])) - jnp.exp(jnp.sum(lp[2] * lp[3])) + lam_init
        ya = diff_attention(qa, ka, va, lam, da_subln[l], lam_init)
        yb = mlstm_branch(p[..., ML_X:ML_O], p[..., ML_O:WA_Q], ml_conv_w[l], ml_conv_b[l],
                          ml_wq[l], ml_wk[l], ml_wv[l], ml_w_gates[l], ml_b_gates[l],
                          ml_norm[l], ml_skip[l])
        qc = apply_rope(p[..., WA_Q:WA_K].reshape(B, S, WA_Q_HEADS, WA_HEAD_DIM), cos, sin)
        kc = apply_rope(p[..., WA_K:WA_V].reshape(B, S, WA_KV_HEADS, WA_HEAD_DIM), cos, sin)
        vc = p[..., WA_V:GATES].reshape(B, S, WA_KV_HEADS, WA_HEAD_DIM)
        yc = window_attention(qc, kc, vc, wa_sinks[l])
        gt = jax.nn.sigmoid(p[..., GATES:].reshape(B, S, N_BRANCH, D_MODEL))
        merged = (gt[:, :, 0] * (ya @ w_proj_a[l]) + gt[:, :, 1] * (yb @ w_proj_b[l])
                  + gt[:, :, 2] * (yc @ w_proj_c[l]))
        x = layer_norm(DN_ALPHA * x + (merged @ w_out[l]).astype(x.dtype), ln1_g[l], ln1_b[l])
        y = moe(x, router_w[l], router_b[l], exp_w_gu[l], exp_b_gu[l], exp_w_down[l], exp_b_down[l])
        x = layer_norm(DN_ALPHA * x + y, ln2_g[l], ln2_b[l])
    return x
```

```python
import functools
import math

import jax
import jax.numpy as jnp
import numpy as np
from jax import lax
from jax.experimental import pallas as pl
from jax.experimental.pallas import tpu as pltpu

F32 = jnp.float32
BF16 = jnp.bfloat16

D_MODEL = 1024
DEPTH = 4
DA_HEADS = 4
HEAD_DIM = 64
ML_HEADS = 4
ML_HEAD_DIM = 128
ML_CONV = 5
ML_CHUNK = 128
WA_Q_HEADS = 8
WA_KV_HEADS = 2
WA_BLOCK = 128
N_EXPERTS = 32
TOP_K = 4
D_FF = D_MODEL
SWIGLU_LIMIT = 7.0
SWIGLU_ALPHA = 1.702
ROPE_THETA = 10000.0
DN_ALPHA = (2.0 * DEPTH) ** 0.25
EPS = 1e-5

LANES = 128
VMEM_LIMIT = 56 * 1024 * 1024

GATE_OFF = 0
QA_OFF = 3072
KA_OFF = 3584
QC_OFF = 4096
KCD_OFF = 4608
VCD_OFF = 4864
VA_OFF = 5120
MLX_OFF = 5632
MLO_OFF = 6144
P_COLS = 6656
PROJ_TN = 512
ROPE_FULL_BLOCKS = (QA_OFF // PROJ_TN, KA_OFF // PROJ_TN, QC_OFF // PROJ_TN)
ROPE_HALF_BLOCK = KCD_OFF // PROJ_TN

_R_DA_Q, _R_DA_K, _R_DA_V = 0, 512, 1024
_R_ML_X, _R_ML_O = 1536, 2048
_R_WA_Q, _R_WA_K, _R_WA_V, _R_GATES = 2560, 3072, 3200, 3328


def _pair_perm():
    return np.concatenate([np.arange(0, 32), np.arange(64, 96), np.arange(32, 64), np.arange(96, 128)])


def _proj_columns():
    perm = _pair_perm()
    idx = np.zeros((P_COLS,), np.int32)
    scale = np.ones((P_COLS,), np.float32)
    idx[GATE_OFF:GATE_OFF + 3072] = _R_GATES + np.arange(3072)
    for t in range(4):
        idx[QA_OFF + t * 128:QA_OFF + (t + 1) * 128] = _R_DA_Q + t * 128 + perm
        idx[KA_OFF + t * 128:KA_OFF + (t + 1) * 128] = _R_DA_K + t * 128 + perm
        idx[QC_OFF + t * 128:QC_OFF + (t + 1) * 128] = _R_WA_Q + t * 128 + perm
    scale[QA_OFF:QA_OFF + 512] = HEAD_DIM ** -0.5
    scale[QC_OFF:QC_OFF + 512] = HEAD_DIM ** -0.5
    dup = np.concatenate([np.arange(64), np.arange(64)])
    for g in range(2):
        idx[KCD_OFF + g * 128:KCD_OFF + (g + 1) * 128] = _R_WA_K + g * 64 + dup[perm]
        idx[VCD_OFF + g * 128:VCD_OFF + (g + 1) * 128] = _R_WA_V + g * 64 + dup
    idx[VA_OFF:VA_OFF + 512] = _R_DA_V + np.arange(512)
    idx[MLX_OFF:MLX_OFF + 512] = _R_ML_X + np.arange(512)
    idx[MLO_OFF:MLO_OFF + 512] = _R_ML_O + np.arange(512)
    return idx, scale


def _cparams(sem, vmem=VMEM_LIMIT):
    return pltpu.CompilerParams(dimension_semantics=sem, vmem_limit_bytes=vmem)


def _dot(a, b):
    return jnp.dot(a, b, preferred_element_type=F32)


def _dot_nt(a, b):
    return lax.dot_general(a, b, (((1,), (1,)), ((), ())), preferred_element_type=F32)


def _split_bf16(x):
    hi = x.astype(BF16)
    lo = (x - hi.astype(F32)).astype(BF16)
    return hi, lo


def _inproj_kernel(x_ref, w_ref, c_ref, s_ref, o_ref, xb_ref):
    j = pl.program_id(1)

    @pl.when(j == 0)
    def _():
        xb_ref[...] = x_ref[...].astype(BF16)

    acc = _dot(xb_ref[...], w_ref[...])

    def rope(a):
        return a * c_ref[...] + pltpu.roll(a, 64, 1) * s_ref[...]

    def tile(t):
        return acc[:, t * LANES:(t + 1) * LANES]

    is_full = (j == ROPE_FULL_BLOCKS[0]) | (j == ROPE_FULL_BLOCKS[1]) | (j == ROPE_FULL_BLOCKS[2])
    is_half = j == ROPE_HALF_BLOCK

    @pl.when(is_full)
    def _():
        o_ref[...] = jnp.concatenate([rope(tile(t)) for t in range(4)], axis=1).astype(BF16)

    @pl.when(is_half)
    def _():
        o_ref[...] = jnp.concatenate([rope(tile(0)), rope(tile(1)), tile(2), tile(3)], axis=1).astype(BF16)

    @pl.when(jnp.logical_not(is_full | is_half))
    def _():
        o_ref[...] = acc.astype(BF16)


def _inproj(x2, w_all, layer, cos_t, sin_t, seq):
    T = x2.shape[0]
    tm = min(1024, seq)
    ns = seq // tm
    return pl.pallas_call(
        _inproj_kernel,
        out_shape=jax.ShapeDtypeStruct((T, P_COLS), BF16),
        grid=(T // tm, P_COLS // PROJ_TN),
        in_specs=[
            pl.BlockSpec((tm, D_MODEL), lambda i, j: (i, 0)),
            pl.BlockSpec((None, D_MODEL, PROJ_TN), lambda i, j: (layer, 0, j)),
            pl.BlockSpec((tm, LANES), lambda i, j: (i % ns, 0)),
            pl.BlockSpec((tm, LANES), lambda i, j: (i % ns, 0)),
        ],
        out_specs=pl.BlockSpec((tm, PROJ_TN), lambda i, j: (i, j)),
        scratch_shapes=[pltpu.VMEM((tm, D_MODEL), BF16)],
        compiler_params=_cparams(("parallel", "arbitrary")),
        name="inproj",
    )(x2, w_all, cos_t, sin_t)


def _da_kernel(lp_ref, q_ref, k_ref, v_ref, sub_ref, o_ref, qs_ref, m_ref, l_ref, acc_ref,
               *, tq, tk, lam_init):
    seq = k_ref.shape[0]
    lane = lax.broadcasted_iota(jnp.int32, (1, LANES), 1)
    map0 = (lane % 64) < 32
    q = q_ref[...].astype(F32)
    qs_ref[0:tq, :] = jnp.where(map0, q, 0.0).astype(BF16)
    qs_ref[tq:2 * tq, :] = jnp.where(map0, 0.0, q).astype(BF16)
    m_ref[...] = jnp.full(m_ref.shape, -jnp.inf, F32)
    l_ref[...] = jnp.zeros(l_ref.shape, F32)
    acc_ref[...] = jnp.zeros(acc_ref.shape, F32)

    def body(c, carry):
        start = pl.multiple_of(c * tk, tk)
        kc = k_ref[pl.ds(start, tk), :]
        vc = v_ref[pl.ds(start, tk), :]
        s = _dot_nt(qs_ref[...], kc)
        m_old = m_ref[...]
        m_new = jnp.maximum(m_old, jnp.max(s, axis=1, keepdims=True))
        alpha = jnp.exp(m_old - m_new)
        p = jnp.exp(s - m_new)
        l_ref[...] = alpha * l_ref[...] + jnp.sum(p, axis=1, keepdims=True)
        acc_ref[...] = alpha * acc_ref[...] + _dot(p.astype(BF16), vc)
        m_ref[...] = m_new
        return carry

    lax.fori_loop(0, seq // tk, body, 0)

    lp = lp_ref[...]
    l01 = jnp.sum(lp[0:1, :] * lp[1:2, :], axis=1, keepdims=True)
    l23 = jnp.sum(lp[2:3, :] * lp[3:4, :], axis=1, keepdims=True)
    lam = jnp.exp(l01) - jnp.exp(l23) + lam_init
    o = acc_ref[...] / l_ref[...]
    a = o[0:tq, :] - lam * o[tq:2 * tq, :]
    ms = jnp.mean(a * a, axis=1, keepdims=True)
    a = a * lax.rsqrt(ms + EPS) * sub_ref[...] * (1.0 - lam_init)
    o_ref[...] = a.astype(BF16)


def _diff_attention(p, lam_params, subln, layer, batch, seq):
    T = p.shape[0]
    tq = min(512, seq)
    tk = min(512, seq)
    nq = seq // tq
    lam_init = 0.8 - 0.6 * math.exp(-0.3 * layer)
    kern = functools.partial(_da_kernel, tq=tq, tk=tk, lam_init=lam_init)
    return pl.pallas_call(
        kern,
        out_shape=jax.ShapeDtypeStruct((T, DA_HEADS * LANES), BF16),
        grid=(batch, DA_HEADS, nq),
        in_specs=[
            pl.BlockSpec((None, 4, HEAD_DIM), lambda b, h, i: (layer, 0, 0)),
            pl.BlockSpec((tq, LANES), lambda b, h, i: (b * nq + i, QA_OFF // LANES + h)),
            pl.BlockSpec((seq, LANES), lambda b, h, i: (b, KA_OFF // LANES + h)),
            pl.BlockSpec((seq, LANES), lambda b, h, i: (b, VA_OFF // LANES + h)),
            pl.BlockSpec((None, 1, LANES), lambda b, h, i: (layer, 0, 0)),
        ],
        out_specs=pl.BlockSpec((tq, LANES), lambda b, h, i: (b * nq + i, h)),
        scratch_shapes=[
            pltpu.VMEM((2 * tq, LANES), BF16),
            pltpu.VMEM((2 * tq, 1), F32),
            pltpu.VMEM((2 * tq, 1), F32),
            pltpu.VMEM((2 * tq, LANES), F32),
        ],
        compiler_params=_cparams(("parallel", "parallel", "arbitrary")),
        name="diff_attn",
    )(lam_params, p, p, p, subln)


def _wa_kernel(sk_ref, q_ref, kp_ref, kc_ref, kn_ref, vp_ref, vc_ref, vn_ref, o_ref, *, nb):
    n = pl.program_id(1)
    blk = WA_BLOCK
    lane = lax.broadcasted_iota(jnp.int32, (1, LANES), 1)
    map0 = (lane % 64) < 32
    qi = lax.broadcasted_iota(jnp.int32, (blk, 3 * blk), 0)
    ki = lax.broadcasted_iota(jnp.int32, (blk, 3 * blk), 1)
    valid = jnp.abs(qi + blk - ki) <= blk
    valid = valid & jnp.logical_not((n == 0) & (ki < blk))
    valid = valid & jnp.logical_not((n == nb - 1) & (ki >= 2 * blk))
    valid4 = jnp.concatenate([valid] * 4, axis=0)
    sk = sk_ref[...]
    for g in range(WA_KV_HEADS):
        gs = slice(g * LANES, (g + 1) * LANES)
        kg = jnp.concatenate([kp_ref[:, gs], kc_ref[:, gs], kn_ref[:, gs]], axis=0)
        vg = jnp.concatenate([vp_ref[:, gs], vc_ref[:, gs], vn_ref[:, gs]], axis=0)
        t0 = q_ref[:, (2 * g) * LANES:(2 * g + 1) * LANES].astype(F32)
        t1 = q_ref[:, (2 * g + 1) * LANES:(2 * g + 2) * LANES].astype(F32)
        qs = jnp.concatenate([jnp.where(map0, t0, 0.0), jnp.where(map0, 0.0, t0),
                              jnp.where(map0, t1, 0.0), jnp.where(map0, 0.0, t1)], axis=0).astype(BF16)
        s = _dot_nt(qs, kg)
        s = jnp.where(valid4, s, -jnp.inf)
        sink = jnp.concatenate(
            [jnp.broadcast_to(sk[:, 4 * g + r:4 * g + r + 1], (blk, 1)) for r in range(4)], axis=0)
        m = jnp.maximum(jnp.max(s, axis=1, keepdims=True), sink)
        e = jnp.exp(s - m)
        pr = e / (jnp.sum(e, axis=1, keepdims=True) + jnp.exp(sink - m))
        o = _dot(pr.astype(BF16), vg)
        left = lane < 64
        o_ref[:, (2 * g) * LANES:(2 * g + 1) * LANES] = jnp.where(
            left, o[0:blk], o[blk:2 * blk]).astype(BF16)
        o_ref[:, (2 * g + 1) * LANES:(2 * g + 2) * LANES] = jnp.where(
            left, o[2 * blk:3 * blk], o[3 * blk:4 * blk]).astype(BF16)


def _window_attention(p, sinks, layer, batch, seq):
    T = p.shape[0]
    nb = seq // WA_BLOCK
    kcol = KCD_OFF // 256
    vcol = VCD_OFF // 256

    def prev(b, n):
        return b * nb + jnp.maximum(n - 1, 0)

    def nxt(b, n):
        return b * nb + jnp.minimum(n + 1, nb - 1)

    kern = functools.partial(_wa_kernel, nb=nb)
    return pl.pallas_call(
        kern,
        out_shape=jax.ShapeDtypeStruct((T, WA_Q_HEADS * HEAD_DIM), BF16),
        grid=(batch, nb),
        in_specs=[
            pl.BlockSpec((None, 1, WA_Q_HEADS), lambda b, n: (layer, 0, 0)),
            pl.BlockSpec((WA_BLOCK, 512), lambda b, n: (b * nb + n, QC_OFF // 512)),
            pl.BlockSpec((WA_BLOCK, 256), lambda b, n: (prev(b, n), kcol)),
            pl.BlockSpec((WA_BLOCK, 256), lambda b, n: (b * nb + n, kcol)),
            pl.BlockSpec((WA_BLOCK, 256), lambda b, n: (nxt(b, n), kcol)),
            pl.BlockSpec((WA_BLOCK, 256), lambda b, n: (prev(b, n), vcol)),
            pl.BlockSpec((WA_BLOCK, 256), lambda b, n: (b * nb + n, vcol)),
            pl.BlockSpec((WA_BLOCK, 256), lambda b, n: (nxt(b, n), vcol)),
        ],
        out_specs=pl.BlockSpec((WA_BLOCK, 512), lambda b, n: (b * nb + n, 0)),
        compiler_params=_cparams(("parallel", "arbitrary")),
        name="window_attn",
    )(sinks, p, p, p, p, p, p, p)


HALO = 16


def _log_sigmoid(x):
    return jnp.minimum(x, 0.0) - jnp.log(1.0 + jnp.exp(-jnp.abs(x)))


def _mlprep_kernel(xp_ref, xc_ref, xn_ref, cw_ref, cb_ref, wqk_ref, wv_ref, wkt_ref, wg_ref, wgt_ref,
                   bg_ref, bgt_ref, q_ref, kt_ref, v_ref, xco_ref, gc_ref, gr_ref, *, tm, ns):
    i = pl.program_id(0)
    first = (i % ns) == 0
    last = (i % ns) == ns - 1
    xm = xc_ref[...].astype(F32)
    xp = jnp.where(first, 0.0, xp_ref[...].astype(F32))
    xn = jnp.where(last, 0.0, xn_ref[...].astype(F32))
    ext = jnp.concatenate([xp, xm, xn], axis=0)
    cw = cw_ref[...]
    conv = cb_ref[...]
    for j in range(ML_CONV):
        off = HALO + j - ML_CONV // 2
        conv = conv + ext[off:off + tm, :] * cw[j:j + 1, :]
    xc = conv * jax.nn.sigmoid(conv)
    xcb = xc.astype(BF16)
    qk = _dot(xcb, wqk_ref[...])
    v = _dot(xc_ref[...], wv_ref[...])
    kt = _dot_nt(wkt_ref[...], xcb)
    gin = jnp.concatenate([qk, v], axis=1).astype(BF16)
    gcol = _dot(gin, wg_ref[...]) + bg_ref[...]
    grow = _dot_nt(wgt_ref[...], gin) + bgt_ref[...]
    chc = lax.broadcasted_iota(jnp.int32, (1, 16), 1)
    chr_ = lax.broadcasted_iota(jnp.int32, (16, 1), 0)
    gcol = jnp.where((chc // 4) % 2 == 1, _log_sigmoid(gcol), gcol)
    grow = jnp.where((chr_ // 4) % 2 == 1, _log_sigmoid(grow), grow)
    q_ref[...] = qk[:, 0:512].astype(BF16)
    kt_ref[...] = (kt * (ML_HEAD_DIM ** -0.5)).astype(BF16)
    v_ref[...] = v.astype(BF16)
    xco_ref[...] = xcb
    gc_ref[...] = gcol
    gr_ref[...] = grow


def _ml_prep(p, conv_w, conv_b, wqk, wv, wkt, wg, wgt, bg, bgt, layer, seq):
    T = p.shape[0]
    tm = min(512, seq)
    ns = seq // tm
    r = tm // HALO
    nh = T // HALO
    kern = functools.partial(_mlprep_kernel, tm=tm, ns=ns)
    mcol = MLX_OFF // 512
    const2 = lambda i: (layer, 0, 0)
    return pl.pallas_call(
        kern,
        out_shape=(
            jax.ShapeDtypeStruct((T, 512), BF16),
            jax.ShapeDtypeStruct((512, T), BF16),
            jax.ShapeDtypeStruct((T, 512), BF16),
            jax.ShapeDtypeStruct((T, 512), BF16),
            jax.ShapeDtypeStruct((T, 16), F32),
            jax.ShapeDtypeStruct((16, T), F32),
        ),
        grid=(T // tm,),
        in_specs=[
            pl.BlockSpec((HALO, 512), lambda i: (jnp.maximum(i * r - 1, 0), mcol)),
            pl.BlockSpec((tm, 512), lambda i: (i, mcol)),
            pl.BlockSpec((HALO, 512), lambda i: (jnp.minimum((i + 1) * r, nh - 1), mcol)),
            pl.BlockSpec((None, ML_CONV, 512), const2),
            pl.BlockSpec((None, 1, 512), const2),
            pl.BlockSpec((None, 512, 1024), const2),
            pl.BlockSpec((None, 512, 512), const2),
            pl.BlockSpec((None, 512, 512), const2),
            pl.BlockSpec((None, 1536, 16), const2),
            pl.BlockSpec((None, 16, 1536), const2),
            pl.BlockSpec((None, 1, 16), const2),
            pl.BlockSpec((None, 16, 1), const2),
        ],
        out_specs=(
            pl.BlockSpec((tm, 512), lambda i: (i, 0)),
            pl.BlockSpec((512, tm), lambda i: (0, i)),
            pl.BlockSpec((tm, 512), lambda i: (i, 0)),
            pl.BlockSpec((tm, 512), lambda i: (i, 0)),
            pl.BlockSpec((tm, 16), lambda i: (i, 0)),
            pl.BlockSpec((16, tm), lambda i: (0, i)),
        ),
        compiler_params=_cparams(("parallel",)),
        name="mlstm_prep",
    )(p, p, p, conv_w, conv_b, wqk, wv, wkt, wg, wgt, bg, bgt)


def _mlscan_kernel(q_ref, kt_ref, v_ref, gc_ref, gr_ref, o_ref, st_ref, m_ref, *, reverse):
    c = pl.program_id(1)
    L = ML_CHUNK

    @pl.when(c == 0)
    def _():
        st_ref[...] = jnp.zeros(st_ref.shape, F32)
        m_ref[...] = jnp.zeros(m_ref.shape, F32)

    ri = lax.broadcasted_iota(jnp.int32, (L, L), 0)
    ci = lax.broadcasted_iota(jnp.int32, (L, L), 1)
    if reverse:
        row_cum = (ri >= ci)
        col_cum = (ci >= ri)
        causal = ci >= ri
    else:
        row_cum = (ri <= ci)
        col_cum = (ci <= ri)
        causal = ci <= ri
    row_m = jnp.where(row_cum, 1.0, 0.0).astype(BF16)
    col_m = jnp.where(col_cum, 1.0, 0.0).astype(BF16)

    gr = gr_ref[...]
    gc = gc_ref[...]
    gr_hi, gr_lo = _split_bf16(gr)
    gc_hi, gc_lo = _split_bf16(gc)
    brow = _dot(gr_hi, row_m) + _dot(gr_lo, row_m)
    bcol = _dot(col_m, gc_hi) + _dot(col_m, gc_lo)
    ones_aug = jnp.ones((L, LANES), BF16)
    base = 8 if reverse else 0
    for h in range(ML_HEADS):
        ich, fch = base + h, base + 4 + h
        b_r = brow[fch:fch + 1, :]
        i_r = gr[ich:ich + 1, :]
        b_c = bcol[:, fch:fch + 1]
        g = b_r[:, 0:1] if reverse else b_r[:, L - 1:L]
        w_end = g - b_r + i_r
        m_loc = jnp.max(w_end, axis=1, keepdims=True)
        e_end = jnp.exp(w_end - m_loc)
        m_prev = m_ref[h][:, 0:1]
        m_new = jnp.maximum(g + m_prev, m_loc)
        a = jnp.exp(g + m_prev - m_new)
        bl = jnp.exp(m_loc - m_new)
        hs = slice(h * LANES, (h + 1) * LANES)
        kt_h = kt_ref[hs, :]
        q_h = q_ref[:, hs]
        v_h = v_ref[:, hs]
        aug = jnp.concatenate([v_h, ones_aug], axis=1)
        c_loc = _dot((kt_h.astype(F32) * e_end).astype(BF16), aug)
        st = st_ref[h]
        q_st = _dot(q_h, st.astype(BF16))
        skq = _dot(q_h, kt_h)
        dm = jnp.where(causal, b_c - b_r + i_r, -jnp.inf)
        a_t = b_c + m_prev
        m_t = jnp.maximum(a_t, jnp.max(dm, axis=1, keepdims=True))
        w = jnp.exp(dm - m_t) * skq
        inter = jnp.exp(a_t - m_t)
        numer = inter * q_st[:, 0:LANES] + _dot(w.astype(BF16), v_h)
        denom = inter * q_st[:, LANES:2 * LANES] + jnp.sum(w, axis=1, keepdims=True)
        o_ref[:, hs] = numer / jnp.maximum(jnp.abs(denom), jnp.exp(-m_t))
        st_ref[h] = a * st + bl * c_loc
        m_ref[h] = jnp.broadcast_to(m_new, (1, LANES))


def _ml_scan(q, kt, v, gc, gr, batch, seq, reverse):
    T = q.shape[0]
    nc = seq // ML_CHUNK

    def chunk(b, c):
        cc = nc - 1 - c if reverse else c
        return b * nc + cc

    kern = functools.partial(_mlscan_kernel, reverse=reverse)
    return pl.pallas_call(
        kern,
        out_shape=jax.ShapeDtypeStruct((T, 512), F32),
        grid=(batch, nc),
        in_specs=[
            pl.BlockSpec((ML_CHUNK, 512), lambda b, c: (chunk(b, c), 0)),
            pl.BlockSpec((512, ML_CHUNK), lambda b, c: (0, chunk(b, c))),
            pl.BlockSpec((ML_CHUNK, 512), lambda b, c: (chunk(b, c), 0)),
            pl.BlockSpec((ML_CHUNK, 16), lambda b, c: (chunk(b, c), 0)),
            pl.BlockSpec((16, ML_CHUNK), lambda b, c: (0, chunk(b, c))),
        ],
        out_specs=pl.BlockSpec((ML_CHUNK, 512), lambda b, c: (chunk(b, c), 0)),
        scratch_shapes=[
            pltpu.VMEM((ML_HEADS, ML_HEAD_DIM, 2 * LANES), F32),
            pltpu.VMEM((ML_HEADS, 1, LANES), F32),
        ],
        compiler_params=_cparams(("parallel", "arbitrary")),
        name="mlstm_scan_bwd" if reverse else "mlstm_scan_fwd",
    )(q, kt, v, gc, gr)


def _layer_norm(x, g, b):
    mu = jnp.mean(x, axis=1, keepdims=True)
    xc = x - mu
    var = jnp.mean(xc * xc, axis=1, keepdims=True)
    return xc * lax.rsqrt(var + EPS) * g + b


def _merge_kernel(x_ref, ya_ref, hf_ref, hb_ref, xc_ref, op_ref, yc_ref, g_ref,
                  nw_ref, sk_ref, wa_ref, wb_ref, wc_ref, wo_ref, lg_ref, lb_ref, rw_ref, rb_ref,
                  x1_ref, ti_ref, tw_ref, rk_ref, cnt_ref, run_ref, *, tm):
    i = pl.program_id(0)

    @pl.when(i == 0)
    def _():
        run_ref[...] = jnp.zeros(run_ref.shape, F32)

    hsum = hf_ref[...] + hb_ref[...]
    parts = []
    for h in range(ML_HEADS):
        hh = hsum[:, h * LANES:(h + 1) * LANES]
        mu = jnp.mean(hh, axis=1, keepdims=True)
        d = hh - mu
        var = jnp.mean(d * d, axis=1, keepdims=True)
        parts.append(d * lax.rsqrt(var + EPS))
    hn = jnp.concatenate(parts, axis=1) * nw_ref[...]
    hn = hn + sk_ref[...] * xc_ref[...].astype(F32)
    yb = (jax.nn.sigmoid(op_ref[...].astype(F32)) * hn).astype(BF16)

    g = g_ref[...].astype(F32)
    merged = (jax.nn.sigmoid(g[:, 0:D_MODEL]) * _dot(ya_ref[...], wa_ref[...])
              + jax.nn.sigmoid(g[:, D_MODEL:2 * D_MODEL]) * _dot(yb, wb_ref[...])
              + jax.nn.sigmoid(g[:, 2 * D_MODEL:3 * D_MODEL]) * _dot(yc_ref[...], wc_ref[...]))
    y = _dot(merged.astype(BF16), wo_ref[...])
    x1 = _layer_norm(DN_ALPHA * x_ref[...] + y, lg_ref[...], lb_ref[...])
    x1_ref[...] = x1

    x_hi, x_lo = _split_bf16(x1)
    rw = rw_ref[...]
    w_hi, w_lo = _split_bf16(rw)
    logits = _dot(x_hi, w_hi) + _dot(x_hi, w_lo) + _dot(x_lo, w_hi) + rb_ref[...]

    eidx = lax.broadcasted_iota(jnp.int32, (tm, N_EXPERTS), 1)
    work = logits
    vals, idxs = [], []
    for _ in range(TOP_K):
        mx = jnp.max(work, axis=1, keepdims=True)
        sel = jnp.min(jnp.where(work == mx, eidx, N_EXPERTS), axis=1, keepdims=True)
        vals.append(mx)
        idxs.append(sel)
        work = jnp.where(eidx == sel, -jnp.inf, work)
    tv = jnp.concatenate(vals, axis=1)
    ti = jnp.concatenate(idxs, axis=1)
    e = jnp.exp(tv - tv[:, 0:1])
    tw_ref[...] = e / jnp.sum(e, axis=1, keepdims=True)
    ti_ref[...] = ti

    onehots = [(eidx == idxs[k]) for k in range(TOP_K)]
    osum = jnp.zeros((tm, N_EXPERTS), F32)
    for k in range(TOP_K):
        osum = osum + jnp.where(onehots[k], 1.0, 0.0)
    ri = lax.broadcasted_iota(jnp.int32, (tm, tm), 0)
    ci = lax.broadcasted_iota(jnp.int32, (tm, tm), 1)
    strict = jnp.where(ci < ri, 1.0, 0.0).astype(BF16)
    before = _dot(strict, osum.astype(BF16)) + run_ref[...]
    ranks = [jnp.sum(jnp.where(onehots[k], before, 0.0), axis=1, keepdims=True) for k in range(TOP_K)]
    rk_ref[...] = jnp.concatenate(ranks, axis=1).astype(jnp.int32)
    run_ref[...] = run_ref[...] + jnp.sum(osum, axis=0, keepdims=True)
    cnt_ref[...] = run_ref[...].astype(jnp.int32)


def _merge(x2, ya, hf, hb, xc, p, yc, norm_w, skip, wa, wb, wc, wo, ln_g, ln_b, rw, rb, layer):
    T = x2.shape[0]
    tm = min(256, T)
    kern = functools.partial(_merge_kernel, tm=tm)
    row = lambda i: (i, 0)
    cw = lambda i: (layer, 0, 0)
    return pl.pallas_call(
        kern,
        out_shape=(
            jax.ShapeDtypeStruct((T, D_MODEL), F32),
            jax.ShapeDtypeStruct((T, TOP_K), jnp.int32),
            jax.ShapeDtypeStruct((T, TOP_K), F32),
            jax.ShapeDtypeStruct((T, TOP_K), jnp.int32),
            jax.ShapeDtypeStruct((1, N_EXPERTS), jnp.int32),
        ),
        grid=(T // tm,),
        in_specs=[
            pl.BlockSpec((tm, D_MODEL), row),
            pl.BlockSpec((tm, 512), row),
            pl.BlockSpec((tm, 512), row),
            pl.BlockSpec((tm, 512), row),
            pl.BlockSpec((tm, 512), row),
            pl.BlockSpec((tm, 512), lambda i: (i, MLO_OFF // 512)),
            pl.BlockSpec((tm, 512), row),
            pl.BlockSpec((tm, 3 * D_MODEL), lambda i: (i, 0)),
            pl.BlockSpec((None, 1, 512), cw),
            pl.BlockSpec((None, 1, 512), cw),
            pl.BlockSpec((None, 512, D_MODEL), cw),
            pl.BlockSpec((None, 512, D_MODEL), cw),
            pl.BlockSpec((None, 512, D_MODEL), cw),
            pl.BlockSpec((None, D_MODEL, D_MODEL), cw),
            pl.BlockSpec((None, 1, D_MODEL), cw),
            pl.BlockSpec((None, 1, D_MODEL), cw),
            pl.BlockSpec((None, D_MODEL, N_EXPERTS), cw),
            pl.BlockSpec((None, 1, N_EXPERTS), cw),
        ],
        out_specs=(
            pl.BlockSpec((tm, D_MODEL), row),
            pl.BlockSpec((tm, TOP_K), row),
            pl.BlockSpec((tm, TOP_K), row),
            pl.BlockSpec((tm, TOP_K), row),
            pl.BlockSpec((1, N_EXPERTS), lambda i: (0, 0)),
        ),
        scratch_shapes=[pltpu.VMEM((1, N_EXPERTS), F32)],
        compiler_params=_cparams(("arbitrary",)),
        name="merge_router",
    )(x2, ya, hf, hb, xc, p, yc, p, norm_w, skip, wa, wb, wc, wo, ln_g, ln_b, rw, rb)


def _scatter_kernel(pos_ref, x_ref, init_ref, o_ref, sem, *, tm):
    del init_ref

    def copy(r, k):
        dst = pos_ref[0, 0, r * TOP_K + k]
        return pltpu.make_async_copy(x_ref.at[pl.ds(r, 1), :], o_ref.at[pl.ds(dst, 1), :], sem)

    def issue(r, carry):
        for k in range(TOP_K):
            copy(r, k).start()
        return carry

    lax.fori_loop(0, tm, issue, 0)

    def drain(r, carry):
        for k in range(TOP_K):
            copy(r, k).wait()
        return carry

    lax.fori_loop(0, tm, drain, 0)


def _scatter_rows(pos_flat, x1, rows_pad):
    T = x1.shape[0]
    tm = min(256, T)
    init = jnp.zeros((rows_pad, D_MODEL), F32)
    kern = functools.partial(_scatter_kernel, tm=tm)
    return pl.pallas_call(
        kern,
        out_shape=jax.ShapeDtypeStruct((rows_pad, D_MODEL), F32),
        grid=(T // tm,),
        in_specs=[
            pl.BlockSpec((1, 1, tm * TOP_K), lambda i: (i, 0, 0), memory_space=pltpu.SMEM),
            pl.BlockSpec((tm, D_MODEL), lambda i: (i, 0)),
            pl.BlockSpec(memory_space=pl.ANY),
        ],
        out_specs=pl.BlockSpec(memory_space=pl.ANY),
        scratch_shapes=[pltpu.SemaphoreType.DMA(())],
        input_output_aliases={2: 0},
        compiler_params=_cparams(("arbitrary",)),
        name="moe_scatter",
    )(pos_flat.reshape(T // tm, 1, tm * TOP_K), x1, init)


def _moe_kernel(te_ref, tr_ref, nt_ref, xs_ref, wgu_ref, bgu_ref, wd_ref, bd_ref, o_ref):
    i = pl.program_id(0)

    @pl.when(i < nt_ref[0])
    def _():
        h = _dot(xs_ref[...].astype(BF16), wgu_ref[...]) + bgu_ref[...]
        gate = jnp.minimum(h[:, 0:D_FF], SWIGLU_LIMIT)
        up = jnp.clip(h[:, D_FF:2 * D_FF], -SWIGLU_LIMIT, SWIGLU_LIMIT)
        act = gate * jax.nn.sigmoid(SWIGLU_ALPHA * gate) * (up + 1.0)
        y = _dot(act.astype(BF16), wd_ref[...].astype(BF16)) + bd_ref[...]
        o_ref[...] = y

    @pl.when(i >= nt_ref[0])
    def _():
        o_ref[...] = jnp.zeros(o_ref.shape, F32)


def _moe_experts(tile_expert, tile_row, n_tiles_used, xs, wgu, bgu, wd, bd, layer, tm):
    rows_pad = xs.shape[0]
    n_tiles = rows_pad // tm
    return pl.pallas_call(
        _moe_kernel,
        out_shape=jax.ShapeDtypeStruct((rows_pad, D_MODEL), F32),
        grid_spec=pltpu.PrefetchScalarGridSpec(
            num_scalar_prefetch=3,
            grid=(n_tiles,),
            in_specs=[
                pl.BlockSpec((tm, D_MODEL), lambda i, te, tr, nt: (tr[i], 0)),
                pl.BlockSpec((None, None, D_MODEL, 2 * D_FF), lambda i, te, tr, nt: (layer, te[i], 0, 0)),
                pl.BlockSpec((None, None, 1, 2 * D_FF), lambda i, te, tr, nt: (layer, te[i], 0, 0)),
                pl.BlockSpec((None, None, D_FF, D_MODEL), lambda i, te, tr, nt: (layer, te[i], 0, 0)),
                pl.BlockSpec((None, None, 1, D_MODEL), lambda i, te, tr, nt: (layer, te[i], 0, 0)),
            ],
            out_specs=pl.BlockSpec((tm, D_MODEL), lambda i, te, tr, nt: (i, 0)),
        ),
        compiler_params=_cparams(("arbitrary",)),
        name="moe_experts",
    )(tile_expert, tile_row, n_tiles_used, xs, wgu, bgu, wd, bd)


def _combine_kernel(pos_ref, x_ref, tw_ref, lg_ref, lb_ref, ys_ref, o_ref, buf_ref, sem, *, tm):
    def copy(r, k):
        src = pos_ref[0, 0, r * TOP_K + k]
        return pltpu.make_async_copy(ys_ref.at[pl.ds(src, 1), :], buf_ref.at[k, pl.ds(r, 1), :], sem)

    def issue(r, carry):
        for k in range(TOP_K):
            copy(r, k).start()
        return carry

    lax.fori_loop(0, tm, issue, 0)

    def drain(r, carry):
        for k in range(TOP_K):
            copy(r, k).wait()
        return carry

    lax.fori_loop(0, tm, drain, 0)

    tw = tw_ref[...]
    y = tw[:, 0:1] * buf_ref[0]
    for k in range(1, TOP_K):
        y = y + tw[:, k:k + 1] * buf_ref[k]
    o_ref[...] = _layer_norm(DN_ALPHA * x_ref[...] + y, lg_ref[...], lb_ref[...])


def _combine(pos_flat, x1, tw, ln_g, ln_b, ys, layer):
    T = x1.shape[0]
    tm = min(256, T)
    kern = functools.partial(_combine_kernel, tm=tm)
    return pl.pallas_call(
        kern,
        out_shape=jax.ShapeDtypeStruct((T, D_MODEL), F32),
        grid=(T // tm,),
        in_specs=[
            pl.BlockSpec((1, 1, tm * TOP_K), lambda i: (i, 0, 0), memory_space=pltpu.SMEM),
            pl.BlockSpec((tm, D_MODEL), lambda i: (i, 0)),
            pl.BlockSpec((tm, TOP_K), lambda i: (i, 0)),
            pl.BlockSpec((None, 1, D_MODEL), lambda i: (layer, 0, 0)),
            pl.BlockSpec((None, 1, D_MODEL), lambda i: (layer, 0, 0)),
            pl.BlockSpec(memory_space=pl.ANY),
        ],
        out_specs=pl.BlockSpec((tm, D_MODEL), lambda i: (i, 0)),
        scratch_shapes=[pltpu.VMEM((TOP_K, tm, D_MODEL), F32), pltpu.SemaphoreType.DMA(())],
        compiler_params=_cparams(("arbitrary",)),
        name="moe_combine",
    )(pos_flat.reshape(T // tm, 1, tm * TOP_K), x1, tw, ln_g, ln_b, ys)


def _block_diag(w):
    L, H, d, e = w.shape
    eye = jnp.eye(H, dtype=w.dtype)
    return jnp.einsum('lhde,hg->lhdge', w, eye).reshape(L, H * d, H * e)


def _rope_tables(seq):
    pos = jnp.arange(seq, dtype=F32)
    inv = ROPE_THETA ** (-jnp.arange(0, HEAD_DIM, 2, dtype=F32) / HEAD_DIM)
    ang = pos[:, None] * inv[None, :]
    cos = jnp.tile(jnp.cos(ang), (1, 4))
    sin = jnp.tile(jnp.sin(ang), (1, 4))
    sign = jnp.where(jnp.arange(LANES) < 64, -1.0, 1.0).astype(F32)
    return cos, sin * sign[None, :]


def _routing_tables(top_i, rank, counts, tm, rows_pad):
    cnt = counts[0]
    padded = ((cnt + tm - 1) // tm) * tm
    ends = jnp.cumsum(padded)
    offs = ends - padded
    pos = (offs[top_i] + rank).reshape(-1).astype(jnp.int32)
    n_tiles = rows_pad // tm
    starts = jnp.arange(n_tiles, dtype=jnp.int32) * tm
    used = (ends[-1] // tm).astype(jnp.int32)
    te = jnp.sum((starts[:, None] >= ends[None, :]).astype(jnp.int32), axis=1)
    last = jnp.maximum(used - 1, 0)
    te_last = te[last]
    valid = jnp.arange(n_tiles, dtype=jnp.int32) < used
    tile_expert = jnp.where(valid, te, te_last).astype(jnp.int32)
    tile_row = jnp.where(valid, jnp.arange(n_tiles, dtype=jnp.int32), last).astype(jnp.int32)
    return pos, tile_expert, tile_row, used.reshape(1)


def kernel(x, w_in, da_lambda, da_subln, w_proj_a, ml_conv_w, ml_conv_b, ml_wq, ml_wk, ml_wv,
           ml_w_gates, ml_b_gates, ml_norm, ml_skip, w_proj_b, wa_sinks, w_proj_c, w_out,
           ln1_g, ln1_b, router_w, router_b, exp_w_gu, exp_b_gu, exp_w_down, exp_b_down,
           ln2_g, ln2_b):
    B, S, _ = x.shape
    T = B * S
    L = w_in.shape[0]
    moe_tm = min(512, T)
    rows_pad = T * TOP_K + N_EXPERTS * moe_tm

    col_idx, col_scale = _proj_columns()
    w_all = (w_in[:, :, col_idx] * col_scale[None, None, :]).astype(BF16)
    cos_t, sin_t = _rope_tables(S)
    wqk = jnp.concatenate([_block_diag(ml_wq), _block_diag(ml_wk)], axis=2).astype(BF16)
    wv = _block_diag(ml_wv).astype(BF16)
    wkt = jnp.swapaxes(_block_diag(ml_wk), 1, 2).astype(BF16)
    wg = ml_w_gates.astype(BF16)
    wgt = jnp.swapaxes(ml_w_gates, 1, 2).astype(BF16)
    bg = ml_b_gates[:, None, :]
    bgt = ml_b_gates[:, :, None]
    wgu = jnp.concatenate([exp_w_gu[..., 0::2], exp_w_gu[..., 1::2]], axis=-1).astype(BF16)
    bgu = jnp.concatenate([exp_b_gu[..., 0::2], exp_b_gu[..., 1::2]], axis=-1)[:, :, None, :]
    bd = exp_b_down[:, :, None, :]
    sub = da_subln[:, None, :]
    wa_b, wb_b, wc_b, wo_b = (w.astype(BF16) for w in (w_proj_a, w_proj_b, w_proj_c, w_out))
    r3 = lambda a: a[:, None, :]

    x2 = x.reshape(T, D_MODEL)
    for l in range(L):
        p = _inproj(x2, w_all, l, cos_t, sin_t, S)
        ya = _diff_attention(p, da_lambda, sub, l, B, S)
        yc = _window_attention(p, r3(wa_sinks), l, B, S)
        q, kt, v, xc, gc, gr = _ml_prep(p, ml_conv_w, r3(ml_conv_b), wqk, wv, wkt, wg, wgt, bg, bgt, l, S)
        hf = _ml_scan(q, kt, v, gc, gr, B, S, reverse=False)
        hb = _ml_scan(q, kt, v, gc, gr, B, S, reverse=True)
        x1, top_i, top_w, rank, counts = _merge(
            x2, ya, hf, hb, xc, p, yc, r3(ml_norm), r3(ml_skip), wa_b, wb_b, wc_b, wo_b,
            r3(ln1_g), r3(ln1_b), router_w, r3(router_b), l)
        pos, tile_expert, tile_row, used = _routing_tables(top_i, rank, counts, moe_tm, rows_pad)
        xs = _scatter_rows(pos, x1, rows_pad)
        ys = _moe_experts(tile_expert, tile_row, used, xs, wgu, bgu, exp_w_down, bd, l, moe_tm)
        x2 = _combine(pos, x1, top_w, r3(ln2_g), r3(ln2_b), ys, l)
    return x2.reshape(B, S, D_MODEL)
```

```python
import functools
import math

import jax
import jax.numpy as jnp
import numpy as np
from jax import lax
from jax.experimental import pallas as pl
from jax.experimental.pallas import tpu as pltpu

F32 = jnp.float32
BF16 = jnp.bfloat16

D_MODEL = 1024
DEPTH = 4
DA_HEADS = 4
HEAD_DIM = 64
ML_HEADS = 4
ML_HEAD_DIM = 128
ML_CONV = 5
ML_CHUNK = 128
WA_Q_HEADS = 8
WA_KV_HEADS = 2
WA_BLOCK = 128
N_EXPERTS = 32
TOP_K = 4
D_FF = D_MODEL
SWIGLU_LIMIT = 7.0
SWIGLU_ALPHA = 1.702
ROPE_THETA = 10000.0
DN_ALPHA = (2.0 * DEPTH) ** 0.25
EPS = 1e-5

LANES = 128
VMEM_LIMIT = 56 * 1024 * 1024

GATE_OFF = 0
QA_OFF = 3072
KA_OFF = 3584
QC_OFF = 4096
KCD_OFF = 4608
VCD_OFF = 4864
VA_OFF = 5120
MLX_OFF = 5632
MLO_OFF = 6144
P_COLS = 6656
PROJ_TN = 512
ROPE_FULL_BLOCKS = (QA_OFF // PROJ_TN, KA_OFF // PROJ_TN, QC_OFF // PROJ_TN)
ROPE_HALF_BLOCK = KCD_OFF // PROJ_TN

_R_DA_Q, _R_DA_K, _R_DA_V = 0, 512, 1024
_R_ML_X, _R_ML_O = 1536, 2048
_R_WA_Q, _R_WA_K, _R_WA_V, _R_GATES = 2560, 3072, 3200, 3328


def _pair_perm():
    return np.concatenate([np.arange(0, 32), np.arange(64, 96), np.arange(32, 64), np.arange(96, 128)])


def _proj_columns():
    perm = _pair_perm()
    idx = np.zeros((P_COLS,), np.int32)
    scale = np.ones((P_COLS,), np.float32)
    idx[GATE_OFF:GATE_OFF + 3072] = _R_GATES + np.arange(3072)
    for t in range(4):
        idx[QA_OFF + t * 128:QA_OFF + (t + 1) * 128] = _R_DA_Q + t * 128 + perm
        idx[KA_OFF + t * 128:KA_OFF + (t + 1) * 128] = _R_DA_K + t * 128 + perm
        idx[QC_OFF + t * 128:QC_OFF + (t + 1) * 128] = _R_WA_Q + t * 128 + perm
    scale[QA_OFF:QA_OFF + 512] = HEAD_DIM ** -0.5
    scale[QC_OFF:QC_OFF + 512] = HEAD_DIM ** -0.5
    dup = np.concatenate([np.arange(64), np.arange(64)])
    for g in range(2):
        idx[KCD_OFF + g * 128:KCD_OFF + (g + 1) * 128] = _R_WA_K + g * 64 + dup[perm]
        idx[VCD_OFF + g * 128:VCD_OFF + (g + 1) * 128] = _R_WA_V + g * 64 + dup
    idx[VA_OFF:VA_OFF + 512] = _R_DA_V + np.arange(512)
    idx[MLX_OFF:MLX_OFF + 512] = _R_ML_X + np.arange(512)
    idx[MLO_OFF:MLO_OFF + 512] = _R_ML_O + np.arange(512)
    return idx, scale


def _cparams(sem, vmem=VMEM_LIMIT):
    return pltpu.CompilerParams(dimension_semantics=sem, vmem_limit_bytes=vmem)


def _dot(a, b):
    return jnp.dot(a, b, preferred_element_type=F32)


def _dot_nt(a, b):
    return lax.dot_general(a, b, (((1,), (1,)), ((), ())), preferred_element_type=F32)


def _lane_repeat(x, reps):
    return jnp.concatenate([x] * reps, axis=1)


def _split_bf16(x):
    hi = x.astype(BF16)
    lo = (x - hi.astype(F32)).astype(BF16)
    return hi, lo


def _inproj_kernel(x_ref, w_ref, c_ref, s_ref, o_ref, xb_ref):
    j = pl.program_id(1)

    @pl.when(j == 0)
    def _():
        xb_ref[...] = x_ref[...].astype(BF16)

    def rope(a):
        return a * c_ref[...] + pltpu.roll(a, 64, 1) * s_ref[...]

    is_full = (j == ROPE_FULL_BLOCKS[0]) | (j == ROPE_FULL_BLOCKS[1]) | (j == ROPE_FULL_BLOCKS[2])
    is_half = j == ROPE_HALF_BLOCK

    def roped_store(n_rope):
        acc = _dot(xb_ref[...], w_ref[...])
        for t in range(PROJ_TN // LANES):
            a = acc[:, t * LANES:(t + 1) * LANES]
            o_ref[:, t * LANES:(t + 1) * LANES] = (rope(a) if t < n_rope else a).astype(BF16)

    @pl.when(is_full)
    def _():
        roped_store(4)

    @pl.when(is_half)
    def _():
        roped_store(2)

    @pl.when(jnp.logical_not(is_full | is_half))
    def _():
        o_ref[...] = _dot(xb_ref[...], w_ref[...]).astype(BF16)


def _inproj(x2, w_all, layer, cos_t, sin_t, seq):
    T = x2.shape[0]
    tm = min(1024, seq)
    ns = seq // tm
    return pl.pallas_call(
        _inproj_kernel,
        out_shape=jax.ShapeDtypeStruct((T, P_COLS), BF16),
        grid=(T // tm, P_COLS // PROJ_TN),
        in_specs=[
            pl.BlockSpec((tm, D_MODEL), lambda i, j: (i, 0)),
            pl.BlockSpec((None, D_MODEL, PROJ_TN), lambda i, j: (layer, 0, j)),
            pl.BlockSpec((tm, LANES), lambda i, j: (i % ns, 0)),
            pl.BlockSpec((tm, LANES), lambda i, j: (i % ns, 0)),
        ],
        out_specs=pl.BlockSpec((tm, PROJ_TN), lambda i, j: (i, j)),
        scratch_shapes=[pltpu.VMEM((tm, D_MODEL), BF16)],
        compiler_params=_cparams(("parallel", "arbitrary")),
        name="inproj",
    )(x2, w_all, cos_t, sin_t)


def _da_kernel(lp_ref, q_ref, k_ref, v_ref, sub_ref, o_ref, qs_ref, m_ref, l_ref, acc_ref,
               *, tq, tk, rb, lam_init):
    seq = k_ref.shape[0]
    lane = lax.broadcasted_iota(jnp.int32, (1, LANES), 1)
    map0 = (lane % 64) < 32
    q = q_ref[...].astype(F32)
    qs_ref[0:tq, :] = jnp.where(map0, q, 0.0).astype(BF16)
    qs_ref[tq:2 * tq, :] = jnp.where(map0, 0.0, q).astype(BF16)
    m_ref[...] = jnp.full(m_ref.shape, -jnp.inf, F32)
    l_ref[...] = jnp.zeros(l_ref.shape, F32)
    acc_ref[...] = jnp.zeros(acc_ref.shape, F32)
    reps = tk // LANES

    def body(c, carry):
        start = pl.multiple_of(c * tk, tk)
        kc = k_ref[pl.ds(start, tk), :]
        vc = v_ref[pl.ds(start, tk), :]
        for r0 in range(0, 2 * tq, rb):
            rows = pl.ds(r0, rb)
            s = _dot_nt(qs_ref[rows, :], kc)
            m_old = m_ref[rows, :]
            m_new = jnp.maximum(m_old, jnp.max(s, axis=1, keepdims=True))
            alpha = jnp.exp(m_old - m_new)
            p = jnp.exp(s - _lane_repeat(m_new, reps))
            l_ref[rows, :] = alpha * l_ref[rows, :] + jnp.sum(p, axis=1, keepdims=True)
            acc_ref[rows, :] = alpha * acc_ref[rows, :] + _dot(p.astype(BF16), vc)
            m_ref[rows, :] = m_new
        return carry

    lax.fori_loop(0, seq // tk, body, 0, unroll=4)

    lp = lp_ref[...]
    l01 = jnp.sum(lp[0:1, :] * lp[1:2, :], axis=1, keepdims=True)
    l23 = jnp.sum(lp[2:3, :] * lp[3:4, :], axis=1, keepdims=True)
    lam = jnp.exp(l01) - jnp.exp(l23) + lam_init
    o = acc_ref[...] / l_ref[...]
    a = o[0:tq, :] - lam * o[tq:2 * tq, :]
    ms = jnp.mean(a * a, axis=1, keepdims=True)
    a = a * lax.rsqrt(ms + EPS) * sub_ref[...] * (1.0 - lam_init)
    o_ref[...] = a.astype(BF16)


def _diff_attention(p, lam_params, subln, layer, batch, seq):
    T = p.shape[0]
    tq = min(512, seq)
    tk = min(1024, seq)
    nq = seq // tq
    lam_init = 0.8 - 0.6 * math.exp(-0.3 * layer)
    rb = 2 * tq
    kern = functools.partial(_da_kernel, tq=tq, tk=tk, rb=rb, lam_init=lam_init)
    return pl.pallas_call(
        kern,
        out_shape=jax.ShapeDtypeStruct((T, DA_HEADS * LANES), BF16),
        grid=(batch, DA_HEADS, nq),
        in_specs=[
            pl.BlockSpec((None, 4, HEAD_DIM), lambda b, h, i: (layer, 0, 0)),
            pl.BlockSpec((tq, LANES), lambda b, h, i: (b * nq + i, QA_OFF // LANES + h)),
            pl.BlockSpec((seq, LANES), lambda b, h, i: (b, KA_OFF // LANES + h)),
            pl.BlockSpec((seq, LANES), lambda b, h, i: (b, VA_OFF // LANES + h)),
            pl.BlockSpec((None, 1, LANES), lambda b, h, i: (layer, 0, 0)),
        ],
        out_specs=pl.BlockSpec((tq, LANES), lambda b, h, i: (b * nq + i, h)),
        scratch_shapes=[
            pltpu.VMEM((2 * tq, LANES), BF16),
            pltpu.VMEM((2 * tq, LANES), F32),
            pltpu.VMEM((2 * tq, LANES), F32),
            pltpu.VMEM((2 * tq, LANES), F32),
        ],
        compiler_params=_cparams(("parallel", "parallel", "arbitrary")),
        name="diff_attn",
    )(lam_params, p, p, p, subln)


def _wa_kernel(sk_ref, q_ref, kp_ref, kc_ref, kn_ref, vp_ref, vc_ref, vn_ref, o_ref, *, nb):
    n = pl.program_id(1)
    blk = WA_BLOCK
    lane = lax.broadcasted_iota(jnp.int32, (1, LANES), 1)
    map0 = (lane % 64) < 32
    qi = lax.broadcasted_iota(jnp.int32, (blk, 3 * blk), 0)
    ki = lax.broadcasted_iota(jnp.int32, (blk, 3 * blk), 1)
    valid = jnp.abs(qi + blk - ki) <= blk
    valid = valid & jnp.logical_not((n == 0) & (ki < blk))
    valid = valid & jnp.logical_not((n == nb - 1) & (ki >= 2 * blk))
    valid4 = jnp.concatenate([valid] * 4, axis=0)
    sk = sk_ref[...]
    for g in range(WA_KV_HEADS):
        gs = slice(g * LANES, (g + 1) * LANES)
        kg = jnp.concatenate([kp_ref[:, gs], kc_ref[:, gs], kn_ref[:, gs]], axis=0)
        vg = jnp.concatenate([vp_ref[:, gs], vc_ref[:, gs], vn_ref[:, gs]], axis=0)
        t0 = q_ref[:, (2 * g) * LANES:(2 * g + 1) * LANES].astype(F32)
        t1 = q_ref[:, (2 * g + 1) * LANES:(2 * g + 2) * LANES].astype(F32)
        qs = jnp.concatenate([jnp.where(map0, t0, 0.0), jnp.where(map0, 0.0, t0),
                              jnp.where(map0, t1, 0.0), jnp.where(map0, 0.0, t1)], axis=0).astype(BF16)
        s = _dot_nt(qs, kg)
        s = jnp.where(valid4, s, -jnp.inf)
        sink = jnp.concatenate(
            [jnp.broadcast_to(sk[:, 4 * g + r:4 * g + r + 1], (blk, LANES)) for r in range(4)], axis=0)
        m = jnp.maximum(jnp.max(s, axis=1, keepdims=True), sink)
        e = jnp.exp(s - _lane_repeat(m, 3))
        inv = 1.0 / (jnp.sum(e, axis=1, keepdims=True) + jnp.exp(sink - m))
        pr = e * _lane_repeat(inv, 3)
        o = _dot(pr.astype(BF16), vg)
        left = lane < 64
        o_ref[:, (2 * g) * LANES:(2 * g + 1) * LANES] = jnp.where(
            left, o[0:blk], o[blk:2 * blk]).astype(BF16)
        o_ref[:, (2 * g + 1) * LANES:(2 * g + 2) * LANES] = jnp.where(
            left, o[2 * blk:3 * blk], o[3 * blk:4 * blk]).astype(BF16)


def _window_attention(p, sinks, layer, batch, seq):
    T = p.shape[0]
    nb = seq // WA_BLOCK
    kcol = KCD_OFF // 256
    vcol = VCD_OFF // 256

    def prev(b, n):
        return b * nb + jnp.maximum(n - 1, 0)

    def nxt(b, n):
        return b * nb + jnp.minimum(n + 1, nb - 1)

    kern = functools.partial(_wa_kernel, nb=nb)
    return pl.pallas_call(
        kern,
        out_shape=jax.ShapeDtypeStruct((T, WA_Q_HEADS * HEAD_DIM), BF16),
        grid=(batch, nb),
        in_specs=[
            pl.BlockSpec((None, 1, WA_Q_HEADS), lambda b, n: (layer, 0, 0)),
            pl.BlockSpec((WA_BLOCK, 512), lambda b, n: (b * nb + n, QC_OFF // 512)),
            pl.BlockSpec((WA_BLOCK, 256), lambda b, n: (prev(b, n), kcol)),
            pl.BlockSpec((WA_BLOCK, 256), lambda b, n: (b * nb + n, kcol)),
            pl.BlockSpec((WA_BLOCK, 256), lambda b, n: (nxt(b, n), kcol)),
            pl.BlockSpec((WA_BLOCK, 256), lambda b, n: (prev(b, n), vcol)),
            pl.BlockSpec((WA_BLOCK, 256), lambda b, n: (b * nb + n, vcol)),
            pl.BlockSpec((WA_BLOCK, 256), lambda b, n: (nxt(b, n), vcol)),
        ],
        out_specs=pl.BlockSpec((WA_BLOCK, 512), lambda b, n: (b * nb + n, 0)),
        compiler_params=_cparams(("parallel", "arbitrary")),
        name="window_attn",
    )(sinks, p, p, p, p, p, p, p)


HALO = 16


def _log_sigmoid(x):
    return jnp.minimum(x, 0.0) - jnp.log(1.0 + jnp.exp(-jnp.abs(x)))


def _mlprep_kernel(xp_ref, xc_ref, xn_ref, cw_ref, cb_ref, wqk_ref, wv_ref, wkt_ref, wg_ref, wgt_ref,
                   bg_ref, bgt_ref, q_ref, kt_ref, v_ref, xco_ref, gc_ref, gr_ref, *, tm, ns):
    i = pl.program_id(0)
    first = (i % ns) == 0
    last = (i % ns) == ns - 1
    xm = xc_ref[...].astype(F32)
    xp = jnp.where(first, 0.0, xp_ref[...].astype(F32))
    xn = jnp.where(last, 0.0, xn_ref[...].astype(F32))
    ext = jnp.concatenate([xp, xm, xn], axis=0)
    cw = cw_ref[...]
    conv = cb_ref[...]
    for j in range(ML_CONV):
        off = HALO + j - ML_CONV // 2
        conv = conv + ext[off:off + tm, :] * cw[j:j + 1, :]
    xc = conv * jax.nn.sigmoid(conv)
    xcb = xc.astype(BF16)
    qk = _dot(xcb, wqk_ref[...])
    v = _dot(xc_ref[...], wv_ref[...])
    kt = _dot_nt(wkt_ref[...], xcb)
    gin = jnp.concatenate([qk, v], axis=1).astype(BF16)
    gcol = _dot(gin, wg_ref[...]) + bg_ref[...]
    grow = _dot_nt(wgt_ref[...], gin) + bgt_ref[...]
    chc = lax.broadcasted_iota(jnp.int32, (1, 16), 1)
    chr_ = lax.broadcasted_iota(jnp.int32, (16, 1), 0)
    gcol = jnp.where((chc // 4) % 2 == 1, _log_sigmoid(gcol), gcol)
    grow = jnp.where((chr_ // 4) % 2 == 1, _log_sigmoid(grow), grow)
    q_ref[...] = qk[:, 0:512].astype(BF16)
    kt_ref[...] = (kt * (ML_HEAD_DIM ** -0.5)).astype(BF16)
    v_ref[...] = v.astype(BF16)
    xco_ref[...] = xcb
    gc_ref[...] = gcol
    gr_ref[...] = grow


def _ml_prep(p, conv_w, conv_b, wqk, wv, wkt, wg, wgt, bg, bgt, layer, seq):
    T = p.shape[0]
    tm = min(512, seq)
    ns = seq // tm
    r = tm // HALO
    nh = T // HALO
    kern = functools.partial(_mlprep_kernel, tm=tm, ns=ns)
    mcol = MLX_OFF // 512
    const2 = lambda i: (layer, 0, 0)
    return pl.pallas_call(
        kern,
        out_shape=(
            jax.ShapeDtypeStruct((T, 512), BF16),
            jax.ShapeDtypeStruct((512, T), BF16),
            jax.ShapeDtypeStruct((T, 512), BF16),
            jax.ShapeDtypeStruct((T, 512), BF16),
            jax.ShapeDtypeStruct((T, 16), F32),
            jax.ShapeDtypeStruct((16, T), F32),
        ),
        grid=(T // tm,),
        in_specs=[
            pl.BlockSpec((HALO, 512), lambda i: (jnp.maximum(i * r - 1, 0), mcol)),
            pl.BlockSpec((tm, 512), lambda i: (i, mcol)),
            pl.BlockSpec((HALO, 512), lambda i: (jnp.minimum((i + 1) * r, nh - 1), mcol)),
            pl.BlockSpec((None, ML_CONV, 512), const2),
            pl.BlockSpec((None, 1, 512), const2),
            pl.BlockSpec((None, 512, 1024), const2),
            pl.BlockSpec((None, 512, 512), const2),
            pl.BlockSpec((None, 512, 512), const2),
            pl.BlockSpec((None, 1536, 16), const2),
            pl.BlockSpec((None, 16, 1536), const2),
            pl.BlockSpec((None, 1, 16), const2),
            pl.BlockSpec((None, 16, 1), const2),
        ],
        out_specs=(
            pl.BlockSpec((tm, 512), lambda i: (i, 0)),
            pl.BlockSpec((512, tm), lambda i: (0, i)),
            pl.BlockSpec((tm, 512), lambda i: (i, 0)),
            pl.BlockSpec((tm, 512), lambda i: (i, 0)),
            pl.BlockSpec((tm, 16), lambda i: (i, 0)),
            pl.BlockSpec((16, tm), lambda i: (0, i)),
        ),
        compiler_params=_cparams(("parallel",)),
        name="mlstm_prep",
    )(p, p, p, conv_w, conv_b, wqk, wv, wkt, wg, wgt, bg, bgt)


def _mlscan_kernel(q_ref, kt_ref, v_ref, gc_ref, gr_ref, o_ref, st_ref, m_ref, *, reverse):
    c = pl.program_id(1)
    L = ML_CHUNK

    @pl.when(c == 0)
    def _():
        st_ref[...] = jnp.zeros(st_ref.shape, F32)
        m_ref[...] = jnp.zeros(m_ref.shape, F32)

    ri = lax.broadcasted_iota(jnp.int32, (L, L), 0)
    ci = lax.broadcasted_iota(jnp.int32, (L, L), 1)
    if reverse:
        row_cum = (ri >= ci)
        col_cum = (ci >= ri)
        causal = ci >= ri
    else:
        row_cum = (ri <= ci)
        col_cum = (ci <= ri)
        causal = ci <= ri
    row_m = jnp.where(row_cum, 1.0, 0.0).astype(BF16)
    col_m = jnp.where(col_cum, 1.0, 0.0).astype(BF16)

    gr = gr_ref[...]
    gc = gc_ref[...]
    gr_hi, gr_lo = _split_bf16(gr)
    gc_hi, gc_lo = _split_bf16(gc)
    brow = _dot(gr_hi, row_m) + _dot(gr_lo, row_m)
    bcol = _dot(col_m, gc_hi) + _dot(col_m, gc_lo)
    ones_aug = jnp.ones((L, LANES), BF16)
    base = 8 if reverse else 0
    for h in range(ML_HEADS):
        ich, fch = base + h, base + 4 + h
        b_r = brow[fch:fch + 1, :]
        i_r = gr[ich:ich + 1, :]
        b_c = bcol[:, fch:fch + 1]
        g = b_r[:, 0:1] if reverse else b_r[:, L - 1:L]
        w_end = g - b_r + i_r
        m_loc = jnp.max(w_end, axis=1, keepdims=True)
        e_end = jnp.exp(w_end - m_loc)
        m_prev = m_ref[h][:, 0:1]
        m_new = jnp.maximum(g + m_prev, m_loc)
        a = jnp.exp(g + m_prev - m_new)
        bl = jnp.exp(m_loc - m_new)
        hs = slice(h * LANES, (h + 1) * LANES)
        kt_h = kt_ref[hs, :]
        q_h = q_ref[:, hs]
        v_h = v_ref[:, hs]
        aug = jnp.concatenate([v_h, ones_aug], axis=1)
        c_loc = _dot((kt_h.astype(F32) * e_end).astype(BF16), aug)
        st = st_ref[h]
        q_st = _dot(q_h, st.astype(BF16))
        skq = _dot(q_h, kt_h)
        dm = jnp.where(causal, b_c - b_r + i_r, -jnp.inf)
        a_t = b_c + m_prev
        m_t = jnp.maximum(a_t, jnp.max(dm, axis=1, keepdims=True))
        w = jnp.exp(dm - m_t) * skq
        inter = jnp.exp(a_t - m_t)
        numer = inter * q_st[:, 0:LANES] + _dot(w.astype(BF16), v_h)
        denom = inter * q_st[:, LANES:2 * LANES] + jnp.sum(w, axis=1, keepdims=True)
        o_ref[:, hs] = numer / jnp.maximum(jnp.abs(denom), jnp.exp(-m_t))
        st_ref[h] = a * st + bl * c_loc
        m_ref[h] = jnp.broadcast_to(m_new, (1, LANES))


def _ml_scan(q, kt, v, gc, gr, batch, seq, reverse):
    T = q.shape[0]
    nc = seq // ML_CHUNK

    def chunk(b, c):
        cc = nc - 1 - c if reverse else c
        return b * nc + cc

    kern = functools.partial(_mlscan_kernel, reverse=reverse)
    return pl.pallas_call(
        kern,
        out_shape=jax.ShapeDtypeStruct((T, 512), F32),
        grid=(batch, nc),
        in_specs=[
            pl.BlockSpec((ML_CHUNK, 512), lambda b, c: (chunk(b, c), 0)),
            pl.BlockSpec((512, ML_CHUNK), lambda b, c: (0, chunk(b, c))),
            pl.BlockSpec((ML_CHUNK, 512), lambda b, c: (chunk(b, c), 0)),
            pl.BlockSpec((ML_CHUNK, 16), lambda b, c: (chunk(b, c), 0)),
            pl.BlockSpec((16, ML_CHUNK), lambda b, c: (0, chunk(b, c))),
        ],
        out_specs=pl.BlockSpec((ML_CHUNK, 512), lambda b, c: (chunk(b, c), 0)),
        scratch_shapes=[
            pltpu.VMEM((ML_HEADS, ML_HEAD_DIM, 2 * LANES), F32),
            pltpu.VMEM((ML_HEADS, 1, LANES), F32),
        ],
        compiler_params=_cparams(("parallel", "arbitrary")),
        name="mlstm_scan_bwd" if reverse else "mlstm_scan_fwd",
    )(q, kt, v, gc, gr)


def _layer_norm(x, g, b):
    mu = jnp.mean(x, axis=1, keepdims=True)
    xc = x - mu
    var = jnp.mean(xc * xc, axis=1, keepdims=True)
    return xc * lax.rsqrt(var + EPS) * g + b


def _merge_kernel(x_ref, ya_ref, hf_ref, hb_ref, xc_ref, op_ref, yc_ref, g_ref,
                  nw_ref, sk_ref, wa_ref, wb_ref, wc_ref, wo_ref, lg_ref, lb_ref, rw_ref, rb_ref,
                  x1_ref, ti_ref, tw_ref, rk_ref, cnt_ref, run_ref, *, tm):
    i = pl.program_id(0)

    @pl.when(i == 0)
    def _():
        run_ref[...] = jnp.zeros(run_ref.shape, F32)

    hsum = hf_ref[...] + hb_ref[...]
    parts = []
    for h in range(ML_HEADS):
        hh = hsum[:, h * LANES:(h + 1) * LANES]
        mu = jnp.mean(hh, axis=1, keepdims=True)
        d = hh - mu
        var = jnp.mean(d * d, axis=1, keepdims=True)
        parts.append(d * lax.rsqrt(var + EPS))
    hn = jnp.concatenate(parts, axis=1) * nw_ref[...]
    hn = hn + sk_ref[...] * xc_ref[...].astype(F32)
    yb = (jax.nn.sigmoid(op_ref[...].astype(F32)) * hn).astype(BF16)

    g = g_ref[...].astype(F32)
    merged = (jax.nn.sigmoid(g[:, 0:D_MODEL]) * _dot(ya_ref[...], wa_ref[...])
              + jax.nn.sigmoid(g[:, D_MODEL:2 * D_MODEL]) * _dot(yb, wb_ref[...])
              + jax.nn.sigmoid(g[:, 2 * D_MODEL:3 * D_MODEL]) * _dot(yc_ref[...], wc_ref[...]))
    y = _dot(merged.astype(BF16), wo_ref[...])
    x1 = _layer_norm(DN_ALPHA * x_ref[...] + y, lg_ref[...], lb_ref[...])
    x1_ref[...] = x1

    x_hi, x_lo = _split_bf16(x1)
    rw = rw_ref[...]
    w_hi, w_lo = _split_bf16(rw)
    logits = _dot(x_hi, w_hi) + _dot(x_hi, w_lo) + _dot(x_lo, w_hi) + rb_ref[...]

    eidx = lax.broadcasted_iota(jnp.int32, (tm, N_EXPERTS), 1)
    work = logits
    vals, idxs = [], []
    for _ in range(TOP_K):
        mx = jnp.max(work, axis=1, keepdims=True)
        sel = jnp.min(jnp.where(work == mx, eidx, N_EXPERTS), axis=1, keepdims=True)
        vals.append(mx)
        idxs.append(sel)
        work = jnp.where(eidx == sel, -jnp.inf, work)
    tv = jnp.concatenate(vals, axis=1)
    ti = jnp.concatenate(idxs, axis=1)
    e = jnp.exp(tv - tv[:, 0:1])
    tw_ref[...] = e / jnp.sum(e, axis=1, keepdims=True)
    ti_ref[...] = ti

    onehots = [(eidx == idxs[k]) for k in range(TOP_K)]
    osum = jnp.zeros((tm, N_EXPERTS), F32)
    for k in range(TOP_K):
        osum = osum + jnp.where(onehots[k], 1.0, 0.0)
    ri = lax.broadcasted_iota(jnp.int32, (tm, tm), 0)
    ci = lax.broadcasted_iota(jnp.int32, (tm, tm), 1)
    strict = jnp.where(ci < ri, 1.0, 0.0).astype(BF16)
    before = _dot(strict, osum.astype(BF16)) + run_ref[...]
    ranks = [jnp.sum(jnp.where(onehots[k], before, 0.0), axis=1, keepdims=True) for k in range(TOP_K)]
    rk_ref[...] = jnp.concatenate(ranks, axis=1).astype(jnp.int32)
    run_ref[...] = run_ref[...] + jnp.sum(osum, axis=0, keepdims=True)
    cnt_ref[...] = run_ref[...].astype(jnp.int32)


def _merge(x2, ya, hf, hb, xc, p, yc, norm_w, skip, wa, wb, wc, wo, ln_g, ln_b, rw, rb, layer):
    T = x2.shape[0]
    tm = min(256, T)
    kern = functools.partial(_merge_kernel, tm=tm)
    row = lambda i: (i, 0)
    cw = lambda i: (layer, 0, 0)
    return pl.pallas_call(
        kern,
        out_shape=(
            jax.ShapeDtypeStruct((T, D_MODEL), F32),
            jax.ShapeDtypeStruct((T, TOP_K), jnp.int32),
            jax.ShapeDtypeStruct((T, TOP_K), F32),
            jax.ShapeDtypeStruct((T, TOP_K), jnp.int32),
            jax.ShapeDtypeStruct((1, N_EXPERTS), jnp.int32),
        ),
        grid=(T // tm,),
        in_specs=[
            pl.BlockSpec((tm, D_MODEL), row),
            pl.BlockSpec((tm, 512), row),
            pl.BlockSpec((tm, 512), row),
            pl.BlockSpec((tm, 512), row),
            pl.BlockSpec((tm, 512), row),
            pl.BlockSpec((tm, 512), lambda i: (i, MLO_OFF // 512)),
            pl.BlockSpec((tm, 512), row),
            pl.BlockSpec((tm, 3 * D_MODEL), lambda i: (i, 0)),
            pl.BlockSpec((None, 1, 512), cw),
            pl.BlockSpec((None, 1, 512), cw),
            pl.BlockSpec((None, 512, D_MODEL), cw),
            pl.BlockSpec((None, 512, D_MODEL), cw),
            pl.BlockSpec((None, 512, D_MODEL), cw),
            pl.BlockSpec((None, D_MODEL, D_MODEL), cw),
            pl.BlockSpec((None, 1, D_MODEL), cw),
            pl.BlockSpec((None, 1, D_MODEL), cw),
            pl.BlockSpec((None, D_MODEL, N_EXPERTS), cw),
            pl.BlockSpec((None, 1, N_EXPERTS), cw),
        ],
        out_specs=(
            pl.BlockSpec((tm, D_MODEL), row),
            pl.BlockSpec((tm, TOP_K), row),
            pl.BlockSpec((tm, TOP_K), row),
            pl.BlockSpec((tm, TOP_K), row),
            pl.BlockSpec((1, N_EXPERTS), lambda i: (0, 0)),
        ),
        scratch_shapes=[pltpu.VMEM((1, N_EXPERTS), F32)],
        compiler_params=_cparams(("arbitrary",)),
        name="merge_router",
    )(x2, ya, hf, hb, xc, p, yc, p, norm_w, skip, wa, wb, wc, wo, ln_g, ln_b, rw, rb)


def _scatter_kernel(pos_ref, x_ref, init_ref, o_ref, sem, *, tm):
    del init_ref

    def copy(r, k):
        dst = pos_ref[0, 0, r * TOP_K + k]
        return pltpu.make_async_copy(x_ref.at[pl.ds(r, 1), :], o_ref.at[pl.ds(dst, 1), :], sem)

    def issue(r, carry):
        for k in range(TOP_K):
            copy(r, k).start()
        return carry

    lax.fori_loop(0, tm, issue, 0, unroll=4)
    for k in range(TOP_K):
        pltpu.make_async_copy(x_ref, o_ref.at[pl.ds(0, tm), :], sem).wait()


def _scatter_rows(pos_flat, x1, rows_pad):
    T = x1.shape[0]
    tm = min(256, T)
    init = jnp.zeros((rows_pad, D_MODEL), F32)
    kern = functools.partial(_scatter_kernel, tm=tm)
    return pl.pallas_call(
        kern,
        out_shape=jax.ShapeDtypeStruct((rows_pad, D_MODEL), F32),
        grid=(T // tm,),
        in_specs=[
            pl.BlockSpec((1, 1, tm * TOP_K), lambda i: (i, 0, 0), memory_space=pltpu.SMEM),
            pl.BlockSpec((tm, D_MODEL), lambda i: (i, 0)),
            pl.BlockSpec(memory_space=pl.ANY),
        ],
        out_specs=pl.BlockSpec(memory_space=pl.ANY),
        scratch_shapes=[pltpu.SemaphoreType.DMA(())],
        input_output_aliases={2: 0},
        compiler_params=_cparams(("arbitrary",)),
        name="moe_scatter",
    )(pos_flat.reshape(T // tm, 1, tm * TOP_K), x1, init)


GU_GROUP = 2 * LANES


def _regroup_kernel(w_ref, o_ref):
    ri = lax.broadcasted_iota(jnp.int32, (GU_GROUP, GU_GROUP), 0)
    ci = lax.broadcasted_iota(jnp.int32, (GU_GROUP, GU_GROUP), 1)
    src = jnp.where(ci < LANES, 2 * ci, 2 * (ci - LANES) + 1)
    perm = jnp.where(ri == src, 1.0, 0.0).astype(BF16)
    for b in range(2 * D_FF // GU_GROUP):
        cols = slice(b * GU_GROUP, (b + 1) * GU_GROUP)
        o_ref[:, cols] = _dot(w_ref[:, cols].astype(BF16), perm).astype(BF16)


def _regroup_gate_up(w_gu):
    L, E = w_gu.shape[0], w_gu.shape[1]
    tr = 512
    return pl.pallas_call(
        _regroup_kernel,
        out_shape=jax.ShapeDtypeStruct((L, E, D_MODEL, 2 * D_FF), BF16),
        grid=(L, E, D_MODEL // tr),
        in_specs=[pl.BlockSpec((None, None, tr, 2 * D_FF), lambda l, e, r: (l, e, r, 0))],
        out_specs=pl.BlockSpec((None, None, tr, 2 * D_FF), lambda l, e, r: (l, e, r, 0)),
        compiler_params=_cparams(("parallel", "parallel", "parallel")),
        name="regroup_gate_up",
    )(w_gu)


def _regroup_bias(b_gu):
    lead = b_gu.shape[:-1]
    g = b_gu.reshape(lead + (2 * D_FF // GU_GROUP, LANES, 2))
    return jnp.swapaxes(g, -1, -2).reshape(lead + (2 * D_FF,))


def _moe_kernel(te_ref, tr_ref, nt_ref, xs_ref, wgu_ref, bgu_ref, wd_ref, bd_ref, o_ref):
    i = pl.program_id(0)

    @pl.when(i < nt_ref[0])
    def _():
        h = _dot(xs_ref[...].astype(BF16), wgu_ref[...]) + bgu_ref[...]
        acts = []
        for b in range(D_FF // LANES):
            gate = jnp.minimum(h[:, 2 * b * LANES:(2 * b + 1) * LANES], SWIGLU_LIMIT)
            up = jnp.clip(h[:, (2 * b + 1) * LANES:(2 * b + 2) * LANES], -SWIGLU_LIMIT, SWIGLU_LIMIT)
            acts.append((gate * jax.nn.sigmoid(SWIGLU_ALPHA * gate) * (up + 1.0)).astype(BF16))
        act = jnp.concatenate(acts, axis=1)
        y = _dot(act, wd_ref[...].astype(BF16)) + bd_ref[...]
        o_ref[...] = y

    @pl.when(i >= nt_ref[0])
    def _():
        o_ref[...] = jnp.zeros(o_ref.shape, F32)


def _moe_experts(tile_expert, tile_row, n_tiles_used, xs, wgu, bgu, wd, bd, layer, tm):
    rows_pad = xs.shape[0]
    n_tiles = rows_pad // tm
    return pl.pallas_call(
        _moe_kernel,
        out_shape=jax.ShapeDtypeStruct((rows_pad, D_MODEL), F32),
        grid_spec=pltpu.PrefetchScalarGridSpec(
            num_scalar_prefetch=3,
            grid=(n_tiles,),
            in_specs=[
                pl.BlockSpec((tm, D_MODEL), lambda i, te, tr, nt: (tr[i], 0)),
                pl.BlockSpec((None, None, D_MODEL, 2 * D_FF), lambda i, te, tr, nt: (layer, te[i], 0, 0)),
                pl.BlockSpec((None, None, 1, 2 * D_FF), lambda i, te, tr, nt: (layer, te[i], 0, 0)),
                pl.BlockSpec((None, None, D_FF, D_MODEL), lambda i, te, tr, nt: (layer, te[i], 0, 0)),
                pl.BlockSpec((None, None, 1, D_MODEL), lambda i, te, tr, nt: (layer, te[i], 0, 0)),
            ],
            out_specs=pl.BlockSpec((tm, D_MODEL), lambda i, te, tr, nt: (i, 0)),
        ),
        compiler_params=_cparams(("arbitrary",)),
        name="moe_experts",
    )(tile_expert, tile_row, n_tiles_used, xs, wgu, bgu, wd, bd)


def _combine_kernel(pos_ref, x_ref, tw_ref, lg_ref, lb_ref, ys_ref, o_ref, buf_ref, sem, *, tm):
    def copy(r, k):
        src = pos_ref[0, 0, r * TOP_K + k]
        return pltpu.make_async_copy(ys_ref.at[pl.ds(src, 1), :], buf_ref.at[k, pl.ds(r, 1), :], sem)

    def issue(r, carry):
        for k in range(TOP_K):
            copy(r, k).start()
        return carry

    lax.fori_loop(0, tm, issue, 0, unroll=4)
    for k in range(TOP_K):
        pltpu.make_async_copy(ys_ref.at[pl.ds(0, tm), :], buf_ref.at[k], sem).wait()

    tw = tw_ref[...]
    y = tw[:, 0:1] * buf_ref[0]
    for k in range(1, TOP_K):
        y = y + tw[:, k:k + 1] * buf_ref[k]
    o_ref[...] = _layer_norm(DN_ALPHA * x_ref[...] + y, lg_ref[...], lb_ref[...])


def _combine(pos_flat, x1, tw, ln_g, ln_b, ys, layer):
    T = x1.shape[0]
    tm = min(256, T)
    kern = functools.partial(_combine_kernel, tm=tm)
    return pl.pallas_call(
        kern,
        out_shape=jax.ShapeDtypeStruct((T, D_MODEL), F32),
        grid=(T // tm,),
        in_specs=[
            pl.BlockSpec((1, 1, tm * TOP_K), lambda i: (i, 0, 0), memory_space=pltpu.SMEM),
            pl.BlockSpec((tm, D_MODEL), lambda i: (i, 0)),
            pl.BlockSpec((tm, TOP_K), lambda i: (i, 0)),
            pl.BlockSpec((None, 1, D_MODEL), lambda i: (layer, 0, 0)),
            pl.BlockSpec((None, 1, D_MODEL), lambda i: (layer, 0, 0)),
            pl.BlockSpec(memory_space=pl.ANY),
        ],
        out_specs=pl.BlockSpec((tm, D_MODEL), lambda i: (i, 0)),
        scratch_shapes=[pltpu.VMEM((TOP_K, tm, D_MODEL), F32), pltpu.SemaphoreType.DMA(())],
        compiler_params=_cparams(("arbitrary",)),
        name="moe_combine",
    )(pos_flat.reshape(T // tm, 1, tm * TOP_K), x1, tw, ln_g, ln_b, ys)


def _block_diag(w):
    L, H, d, e = w.shape
    eye = jnp.eye(H, dtype=w.dtype)
    return jnp.einsum('lhde,hg->lhdge', w, eye).reshape(L, H * d, H * e)


def _rope_tables(seq):
    pos = jnp.arange(seq, dtype=F32)
    inv = ROPE_THETA ** (-jnp.arange(0, HEAD_DIM, 2, dtype=F32) / HEAD_DIM)
    ang = pos[:, None] * inv[None, :]
    cos = jnp.tile(jnp.cos(ang), (1, 4))
    sin = jnp.tile(jnp.sin(ang), (1, 4))
    sign = jnp.where(jnp.arange(LANES) < 64, -1.0, 1.0).astype(F32)
    return cos, sin * sign[None, :]


def _routing_tables(top_i, rank, counts, tm, rows_pad):
    cnt = counts[0]
    padded = ((cnt + tm - 1) // tm) * tm
    ends = jnp.cumsum(padded)
    offs = ends - padded
    pos = (offs[top_i] + rank).reshape(-1).astype(jnp.int32)
    n_tiles = rows_pad // tm
    starts = jnp.arange(n_tiles, dtype=jnp.int32) * tm
    used = (ends[-1] // tm).astype(jnp.int32)
    te = jnp.sum((starts[:, None] >= ends[None, :]).astype(jnp.int32), axis=1)
    last = jnp.maximum(used - 1, 0)
    te_last = te[last]
    valid = jnp.arange(n_tiles, dtype=jnp.int32) < used
    tile_expert = jnp.where(valid, te, te_last).astype(jnp.int32)
    tile_row = jnp.where(valid, jnp.arange(n_tiles, dtype=jnp.int32), last).astype(jnp.int32)
    return pos, tile_expert, tile_row, used.reshape(1)


def kernel(x, w_in, da_lambda, da_subln, w_proj_a, ml_conv_w, ml_conv_b, ml_wq, ml_wk, ml_wv,
           ml_w_gates, ml_b_gates, ml_norm, ml_skip, w_proj_b, wa_sinks, w_proj_c, w_out,
           ln1_g, ln1_b, router_w, router_b, exp_w_gu, exp_b_gu, exp_w_down, exp_b_down,
           ln2_g, ln2_b):
    B, S, _ = x.shape
    T = B * S
    L = w_in.shape[0]
    moe_tm = min(512, T)
    rows_pad = T * TOP_K + N_EXPERTS * moe_tm

    col_idx, col_scale = _proj_columns()
    w_all = (w_in[:, :, col_idx] * col_scale[None, None, :]).astype(BF16)
    cos_t, sin_t = _rope_tables(S)
    wqk = jnp.concatenate([_block_diag(ml_wq), _block_diag(ml_wk)], axis=2).astype(BF16)
    wv = _block_diag(ml_wv).astype(BF16)
    wkt = jnp.swapaxes(_block_diag(ml_wk), 1, 2).astype(BF16)
    wg = ml_w_gates.astype(BF16)
    wgt = jnp.swapaxes(ml_w_gates, 1, 2).astype(BF16)
    bg = ml_b_gates[:, None, :]
    bgt = ml_b_gates[:, :, None]
    wgu = _regroup_gate_up(exp_w_gu)
    bgu = _regroup_bias(exp_b_gu)[:, :, None, :]
    bd = exp_b_down[:, :, None, :]
    sub = da_subln[:, None, :]
    wa_b, wb_b, wc_b, wo_b = (w.astype(BF16) for w in (w_proj_a, w_proj_b, w_proj_c, w_out))
    r3 = lambda a: a[:, None, :]

    x2 = x.reshape(T, D_MODEL)
    for l in range(L):
        p = _inproj(x2, w_all, l, cos_t, sin_t, S)
        ya = _diff_attention(p, da_lambda, sub, l, B, S)
        yc = _window_attention(p, r3(wa_sinks), l, B, S)
        q, kt, v, xc, gc, gr = _ml_prep(p, ml_conv_w, r3(ml_conv_b), wqk, wv, wkt, wg, wgt, bg, bgt, l, S)
        hf = _ml_scan(q, kt, v, gc, gr, B, S, reverse=False)
        hb = _ml_scan(q, kt, v, gc, gr, B, S, reverse=True)
        x1, top_i, top_w, rank, counts = _merge(
            x2, ya, hf, hb, xc, p, yc, r3(ml_norm), r3(ml_skip), wa_b, wb_b, wc_b, wo_b,
            r3(ln1_g), r3(ln1_b), router_w, r3(router_b), l)
        pos, tile_expert, tile_row, used = _routing_tables(top_i, rank, counts, moe_tm, rows_pad)
        xs = _scatter_rows(pos, x1, rows_pad)
        ys = _moe_experts(tile_expert, tile_row, used, xs, wgu, bgu, exp_w_down, bd, l, moe_tm)
        x2 = _combine(pos, x1, top_w, r3(ln2_g), r3(ln2_b), ys, l)
    return x2.reshape(B, S, D_MODEL)
```

```python
import functools
import math

import jax
import jax.numpy as jnp
import numpy as np
from jax import lax
from jax.experimental import pallas as pl
from jax.experimental.pallas import tpu as pltpu

F32 = jnp.float32
BF16 = jnp.bfloat16

D_MODEL = 1024
DEPTH = 4
DA_HEADS = 4
HEAD_DIM = 64
ML_HEADS = 4
ML_HEAD_DIM = 128
ML_CONV = 5
ML_CHUNK = 128
WA_Q_HEADS = 8
WA_KV_HEADS = 2
WA_BLOCK = 128
N_EXPERTS = 32
TOP_K = 4
D_FF = D_MODEL
SWIGLU_LIMIT = 7.0
SWIGLU_ALPHA = 1.702
ROPE_THETA = 10000.0
DN_ALPHA = (2.0 * DEPTH) ** 0.25
EPS = 1e-5

LANES = 128
VMEM_LIMIT = 56 * 1024 * 1024

GATE_OFF = 0
QA_OFF = 3072
KA_OFF = 3584
QC_OFF = 4096
KCD_OFF = 4608
VCD_OFF = 4864
VA_OFF = 5120
MLX_OFF = 5632
MLO_OFF = 6144
P_COLS = 6656
PROJ_TN = 512
ROPE_FULL_BLOCKS = (QA_OFF // PROJ_TN, KA_OFF // PROJ_TN, QC_OFF // PROJ_TN)
ROPE_HALF_BLOCK = KCD_OFF // PROJ_TN

_R_DA_Q, _R_DA_K, _R_DA_V = 0, 512, 1024
_R_ML_X, _R_ML_O = 1536, 2048
_R_WA_Q, _R_WA_K, _R_WA_V, _R_GATES = 2560, 3072, 3200, 3328


def _pair_perm():
    return np.concatenate([np.arange(0, 32), np.arange(64, 96), np.arange(32, 64), np.arange(96, 128)])


def _proj_columns():
    perm = _pair_perm()
    idx = np.zeros((P_COLS,), np.int32)
    scale = np.ones((P_COLS,), np.float32)
    idx[GATE_OFF:GATE_OFF + 3072] = _R_GATES + np.arange(3072)
    for t in range(4):
        idx[QA_OFF + t * 128:QA_OFF + (t + 1) * 128] = _R_DA_Q + t * 128 + perm
        idx[KA_OFF + t * 128:KA_OFF + (t + 1) * 128] = _R_DA_K + t * 128 + perm
        idx[QC_OFF + t * 128:QC_OFF + (t + 1) * 128] = _R_WA_Q + t * 128 + perm
    scale[QA_OFF:QA_OFF + 512] = HEAD_DIM ** -0.5
    scale[QC_OFF:QC_OFF + 512] = HEAD_DIM ** -0.5
    dup = np.concatenate([np.arange(64), np.arange(64)])
    for g in range(2):
        idx[KCD_OFF + g * 128:KCD_OFF + (g + 1) * 128] = _R_WA_K + g * 64 + dup[perm]
        idx[VCD_OFF + g * 128:VCD_OFF + (g + 1) * 128] = _R_WA_V + g * 64 + dup
    idx[VA_OFF:VA_OFF + 512] = _R_DA_V + np.arange(512)
    idx[MLX_OFF:MLX_OFF + 512] = _R_ML_X + np.arange(512)
    idx[MLO_OFF:MLO_OFF + 512] = _R_ML_O + np.arange(512)
    return idx, scale


def _cparams(sem, vmem=VMEM_LIMIT):
    return pltpu.CompilerParams(dimension_semantics=sem, vmem_limit_bytes=vmem)


def _dot(a, b):
    return jnp.dot(a, b, preferred_element_type=F32)


def _dot_nt(a, b):
    return lax.dot_general(a, b, (((1,), (1,)), ((), ())), preferred_element_type=F32)


def _lane_repeat(x, reps):
    return jnp.concatenate([x] * reps, axis=1)


def _split_bf16(x):
    hi = x.astype(BF16)
    lo = (x - hi.astype(F32)).astype(BF16)
    return hi, lo


def _inproj_kernel(x_ref, w_ref, c_ref, s_ref, o_ref, xb_ref):
    j = pl.program_id(1)

    @pl.when(j == 0)
    def _():
        xb_ref[...] = x_ref[...].astype(BF16)

    def rope(a):
        return a * c_ref[...] + pltpu.roll(a, 64, 1) * s_ref[...]

    is_full = (j == ROPE_FULL_BLOCKS[0]) | (j == ROPE_FULL_BLOCKS[1]) | (j == ROPE_FULL_BLOCKS[2])
    is_half = j == ROPE_HALF_BLOCK

    def roped_store(n_rope):
        acc = _dot(xb_ref[...], w_ref[...])
        for t in range(PROJ_TN // LANES):
            a = acc[:, t * LANES:(t + 1) * LANES]
            o_ref[:, t * LANES:(t + 1) * LANES] = (rope(a) if t < n_rope else a).astype(BF16)

    @pl.when(is_full)
    def _():
        roped_store(4)

    @pl.when(is_half)
    def _():
        roped_store(2)

    @pl.when(jnp.logical_not(is_full | is_half))
    def _():
        o_ref[...] = _dot(xb_ref[...], w_ref[...]).astype(BF16)


def _inproj(x2, w_all, layer, cos_t, sin_t, seq):
    T = x2.shape[0]
    tm = min(1024, seq)
    ns = seq // tm
    return pl.pallas_call(
        _inproj_kernel,
        out_shape=jax.ShapeDtypeStruct((T, P_COLS), BF16),
        grid=(T // tm, P_COLS // PROJ_TN),
        in_specs=[
            pl.BlockSpec((tm, D_MODEL), lambda i, j: (i, 0)),
            pl.BlockSpec((None, D_MODEL, PROJ_TN), lambda i, j: (layer, 0, j)),
            pl.BlockSpec((tm, LANES), lambda i, j: (i % ns, 0)),
            pl.BlockSpec((tm, LANES), lambda i, j: (i % ns, 0)),
        ],
        out_specs=pl.BlockSpec((tm, PROJ_TN), lambda i, j: (i, j)),
        scratch_shapes=[pltpu.VMEM((tm, D_MODEL), BF16)],
        compiler_params=_cparams(("parallel", "arbitrary")),
        name="inproj",
    )(x2, w_all, cos_t, sin_t)


def _da_kernel(lp_ref, q_ref, k_ref, v_ref, sub_ref, o_ref, qs_ref, m_ref, l_ref, acc_ref,
               *, tq, tk, lam_init):
    seq = k_ref.shape[0]
    lane = lax.broadcasted_iota(jnp.int32, (1, LANES), 1)
    map0 = (lane % 64) < 32
    q = q_ref[...].astype(F32)
    qs_ref[0:tq, :] = jnp.where(map0, q, 0.0).astype(BF16)
    qs_ref[tq:2 * tq, :] = jnp.where(map0, 0.0, q).astype(BF16)
    m_ref[...] = jnp.full(m_ref.shape, -jnp.inf, F32)
    l_ref[...] = jnp.zeros(l_ref.shape, F32)
    acc_ref[...] = jnp.zeros(acc_ref.shape, F32)
    reps = tk // LANES

    def body(c, carry):
        start = pl.multiple_of(c * tk, tk)
        s = _dot_nt(qs_ref[...], k_ref[pl.ds(start, tk), :])
        m_old = m_ref[...]
        m_new = jnp.maximum(m_old, jnp.max(s, axis=1, keepdims=True))
        alpha = jnp.exp(m_old - m_new)
        p = jnp.exp(s - _lane_repeat(m_new, reps))
        l_ref[...] = alpha * l_ref[...] + jnp.sum(p, axis=1, keepdims=True)
        acc_ref[...] = alpha * acc_ref[...] + _dot(p.astype(BF16), v_ref[pl.ds(start, tk), :])
        m_ref[...] = m_new
        return carry

    lax.fori_loop(0, seq // tk, body, 0, unroll=4)

    lp = lp_ref[...]
    l01 = jnp.sum(lp[0:1, :] * lp[1:2, :], axis=1, keepdims=True)
    l23 = jnp.sum(lp[2:3, :] * lp[3:4, :], axis=1, keepdims=True)
    lam = jnp.exp(l01) - jnp.exp(l23) + lam_init
    o = acc_ref[...] / l_ref[...]
    a = o[0:tq, :] - lam * o[tq:2 * tq, :]
    ms = jnp.mean(a * a, axis=1, keepdims=True)
    a = a * lax.rsqrt(ms + EPS) * sub_ref[...] * (1.0 - lam_init)
    o_ref[...] = a.astype(BF16)


def _diff_attention(p, lam_params, subln, layer, batch, seq):
    T = p.shape[0]
    tq = min(512, seq)
    tk = min(1024, seq)
    nq = seq // tq
    lam_init = 0.8 - 0.6 * math.exp(-0.3 * layer)
    kern = functools.partial(_da_kernel, tq=tq, tk=tk, lam_init=lam_init)
    return pl.pallas_call(
        kern,
        out_shape=jax.ShapeDtypeStruct((T, DA_HEADS * LANES), BF16),
        grid=(batch, DA_HEADS, nq),
        in_specs=[
            pl.BlockSpec((None, 4, HEAD_DIM), lambda b, h, i: (layer, 0, 0)),
            pl.BlockSpec((tq, LANES), lambda b, h, i: (b * nq + i, QA_OFF // LANES + h)),
            pl.BlockSpec((seq, LANES), lambda b, h, i: (b, KA_OFF // LANES + h)),
            pl.BlockSpec((seq, LANES), lambda b, h, i: (b, VA_OFF // LANES + h)),
            pl.BlockSpec((None, 1, LANES), lambda b, h, i: (layer, 0, 0)),
        ],
        out_specs=pl.BlockSpec((tq, LANES), lambda b, h, i: (b * nq + i, h)),
        scratch_shapes=[
            pltpu.VMEM((2 * tq, LANES), BF16),
            pltpu.VMEM((2 * tq, LANES), F32),
            pltpu.VMEM((2 * tq, LANES), F32),
            pltpu.VMEM((2 * tq, LANES), F32),
        ],
        compiler_params=_cparams(("parallel", "parallel", "arbitrary")),
        name="diff_attn",
    )(lam_params, p, p, p, subln)


def _wa_kernel(sk_ref, q_ref, kp_ref, kc_ref, kn_ref, vp_ref, vc_ref, vn_ref, o_ref, *, nb):
    n = pl.program_id(1)
    blk = WA_BLOCK
    lane = lax.broadcasted_iota(jnp.int32, (1, LANES), 1)
    map0 = (lane % 64) < 32
    qi = lax.broadcasted_iota(jnp.int32, (blk, 3 * blk), 0)
    ki = lax.broadcasted_iota(jnp.int32, (blk, 3 * blk), 1)
    valid = jnp.abs(qi + blk - ki) <= blk
    valid = valid & jnp.logical_not((n == 0) & (ki < blk))
    valid = valid & jnp.logical_not((n == nb - 1) & (ki >= 2 * blk))
    valid4 = jnp.concatenate([valid] * 4, axis=0)
    sk = sk_ref[...]
    for g in range(WA_KV_HEADS):
        gs = slice(g * LANES, (g + 1) * LANES)
        kg = jnp.concatenate([kp_ref[:, gs], kc_ref[:, gs], kn_ref[:, gs]], axis=0)
        vg = jnp.concatenate([vp_ref[:, gs], vc_ref[:, gs], vn_ref[:, gs]], axis=0)
        t0 = q_ref[:, (2 * g) * LANES:(2 * g + 1) * LANES].astype(F32)
        t1 = q_ref[:, (2 * g + 1) * LANES:(2 * g + 2) * LANES].astype(F32)
        qs = jnp.concatenate([jnp.where(map0, t0, 0.0), jnp.where(map0, 0.0, t0),
                              jnp.where(map0, t1, 0.0), jnp.where(map0, 0.0, t1)], axis=0).astype(BF16)
        s = _dot_nt(qs, kg)
        s = jnp.where(valid4, s, -jnp.inf)
        sink = jnp.concatenate(
            [jnp.broadcast_to(sk[:, 4 * g + r:4 * g + r + 1], (blk, LANES)) for r in range(4)], axis=0)
        m = jnp.maximum(jnp.max(s, axis=1, keepdims=True), sink)
        e = jnp.exp(s - _lane_repeat(m, 3))
        inv = 1.0 / (jnp.sum(e, axis=1, keepdims=True) + jnp.exp(sink - m))
        pr = e * _lane_repeat(inv, 3)
        o = _dot(pr.astype(BF16), vg)
        left = lane < 64
        o_ref[:, (2 * g) * LANES:(2 * g + 1) * LANES] = jnp.where(
            left, o[0:blk], o[blk:2 * blk]).astype(BF16)
        o_ref[:, (2 * g + 1) * LANES:(2 * g + 2) * LANES] = jnp.where(
            left, o[2 * blk:3 * blk], o[3 * blk:4 * blk]).astype(BF16)


def _window_attention(p, sinks, layer, batch, seq):
    T = p.shape[0]
    nb = seq // WA_BLOCK
    kcol = KCD_OFF // 256
    vcol = VCD_OFF // 256

    def prev(b, n):
        return b * nb + jnp.maximum(n - 1, 0)

    def nxt(b, n):
        return b * nb + jnp.minimum(n + 1, nb - 1)

    kern = functools.partial(_wa_kernel, nb=nb)
    return pl.pallas_call(
        kern,
        out_shape=jax.ShapeDtypeStruct((T, WA_Q_HEADS * HEAD_DIM), BF16),
        grid=(batch, nb),
        in_specs=[
            pl.BlockSpec((None, 1, WA_Q_HEADS), lambda b, n: (layer, 0, 0)),
            pl.BlockSpec((WA_BLOCK, 512), lambda b, n: (b * nb + n, QC_OFF // 512)),
            pl.BlockSpec((WA_BLOCK, 256), lambda b, n: (prev(b, n), kcol)),
            pl.BlockSpec((WA_BLOCK, 256), lambda b, n: (b * nb + n, kcol)),
            pl.BlockSpec((WA_BLOCK, 256), lambda b, n: (nxt(b, n), kcol)),
            pl.BlockSpec((WA_BLOCK, 256), lambda b, n: (prev(b, n), vcol)),
            pl.BlockSpec((WA_BLOCK, 256), lambda b, n: (b * nb + n, vcol)),
            pl.BlockSpec((WA_BLOCK, 256), lambda b, n: (nxt(b, n), vcol)),
        ],
        out_specs=pl.BlockSpec((WA_BLOCK, 512), lambda b, n: (b * nb + n, 0)),
        compiler_params=_cparams(("parallel", "arbitrary")),
        name="window_attn",
    )(sinks, p, p, p, p, p, p, p)


HALO = 16


def _log_sigmoid(x):
    return jnp.minimum(x, 0.0) - jnp.log(1.0 + jnp.exp(-jnp.abs(x)))


def _mlprep_kernel(xp_ref, xc_ref, xn_ref, cw_ref, cb_ref, wqk_ref, wv_ref, wkt_ref, wg_ref, wgt_ref,
                   bg_ref, bgt_ref, q_ref, kt_ref, v_ref, xco_ref, gc_ref, gr_ref, *, tm, ns):
    i = pl.program_id(0)
    first = (i % ns) == 0
    last = (i % ns) == ns - 1
    xm = xc_ref[...].astype(F32)
    xp = jnp.where(first, 0.0, xp_ref[...].astype(F32))
    xn = jnp.where(last, 0.0, xn_ref[...].astype(F32))
    ext = jnp.concatenate([xp, xm, xn], axis=0)
    cw = cw_ref[...]
    conv = cb_ref[...]
    for j in range(ML_CONV):
        off = HALO + j - ML_CONV // 2
        conv = conv + ext[off:off + tm, :] * cw[j:j + 1, :]
    xc = conv * jax.nn.sigmoid(conv)
    xcb = xc.astype(BF16)
    qk = _dot(xcb, wqk_ref[...])
    v = _dot(xc_ref[...], wv_ref[...])
    kt = _dot_nt(wkt_ref[...], xcb)
    gin = jnp.concatenate([qk, v], axis=1).astype(BF16)
    gcol = _dot(gin, wg_ref[...]) + bg_ref[...]
    grow = _dot_nt(wgt_ref[...], gin) + bgt_ref[...]
    chc = lax.broadcasted_iota(jnp.int32, (1, 16), 1)
    chr_ = lax.broadcasted_iota(jnp.int32, (16, 1), 0)
    gcol = jnp.where((chc // 4) % 2 == 1, _log_sigmoid(gcol), gcol)
    grow = jnp.where((chr_ // 4) % 2 == 1, _log_sigmoid(grow), grow)
    q_ref[...] = qk[:, 0:512].astype(BF16)
    kt_ref[...] = (kt * (ML_HEAD_DIM ** -0.5)).astype(BF16)
    v_ref[...] = v.astype(BF16)
    xco_ref[...] = xcb
    gc_ref[...] = gcol
    gr_ref[...] = grow


def _ml_prep(p, conv_w, conv_b, wqk, wv, wkt, wg, wgt, bg, bgt, layer, seq):
    T = p.shape[0]
    tm = min(512, seq)
    ns = seq // tm
    r = tm // HALO
    nh = T // HALO
    kern = functools.partial(_mlprep_kernel, tm=tm, ns=ns)
    mcol = MLX_OFF // 512
    const2 = lambda i: (layer, 0, 0)
    return pl.pallas_call(
        kern,
        out_shape=(
            jax.ShapeDtypeStruct((T, 512), BF16),
            jax.ShapeDtypeStruct((512, T), BF16),
            jax.ShapeDtypeStruct((T, 512), BF16),
            jax.ShapeDtypeStruct((T, 512), BF16),
            jax.ShapeDtypeStruct((T, 16), F32),
            jax.ShapeDtypeStruct((16, T), F32),
        ),
        grid=(T // tm,),
        in_specs=[
            pl.BlockSpec((HALO, 512), lambda i: (jnp.maximum(i * r - 1, 0), mcol)),
            pl.BlockSpec((tm, 512), lambda i: (i, mcol)),
            pl.BlockSpec((HALO, 512), lambda i: (jnp.minimum((i + 1) * r, nh - 1), mcol)),
            pl.BlockSpec((None, ML_CONV, 512), const2),
            pl.BlockSpec((None, 1, 512), const2),
            pl.BlockSpec((None, 512, 1024), const2),
            pl.BlockSpec((None, 512, 512), const2),
            pl.BlockSpec((None, 512, 512), const2),
            pl.BlockSpec((None, 1536, 16), const2),
            pl.BlockSpec((None, 16, 1536), const2),
            pl.BlockSpec((None, 1, 16), const2),
            pl.BlockSpec((None, 16, 1), const2),
        ],
        out_specs=(
            pl.BlockSpec((tm, 512), lambda i: (i, 0)),
            pl.BlockSpec((512, tm), lambda i: (0, i)),
            pl.BlockSpec((tm, 512), lambda i: (i, 0)),
            pl.BlockSpec((tm, 512), lambda i: (i, 0)),
            pl.BlockSpec((tm, 16), lambda i: (i, 0)),
            pl.BlockSpec((16, tm), lambda i: (0, i)),
        ),
        compiler_params=_cparams(("parallel",)),
        name="mlstm_prep",
    )(p, p, p, conv_w, conv_b, wqk, wv, wkt, wg, wgt, bg, bgt)


def _mlscan_kernel(q_ref, kt_ref, v_ref, gc_ref, gr_ref, o_ref, st_ref, m_ref, *, reverse):
    c = pl.program_id(1)
    L = ML_CHUNK

    @pl.when(c == 0)
    def _():
        st_ref[...] = jnp.zeros(st_ref.shape, F32)
        m_ref[...] = jnp.zeros(m_ref.shape, F32)

    ri = lax.broadcasted_iota(jnp.int32, (L, L), 0)
    ci = lax.broadcasted_iota(jnp.int32, (L, L), 1)
    if reverse:
        row_cum = (ri >= ci)
        col_cum = (ci >= ri)
        causal = ci >= ri
    else:
        row_cum = (ri <= ci)
        col_cum = (ci <= ri)
        causal = ci <= ri
    row_m = jnp.where(row_cum, 1.0, 0.0).astype(BF16)
    col_m = jnp.where(col_cum, 1.0, 0.0).astype(BF16)

    gr = gr_ref[...]
    gc = gc_ref[...]
    gr_hi, gr_lo = _split_bf16(gr)
    gc_hi, gc_lo = _split_bf16(gc)
    brow = _dot(gr_hi, row_m) + _dot(gr_lo, row_m)
    bcol = _dot(col_m, gc_hi) + _dot(col_m, gc_lo)
    ones_aug = jnp.ones((L, LANES), BF16)
    base = 8 if reverse else 0
    for h in range(ML_HEADS):
        ich, fch = base + h, base + 4 + h
        b_r = brow[fch:fch + 1, :]
        i_r = gr[ich:ich + 1, :]
        b_c = bcol[:, fch:fch + 1]
        g = b_r[:, 0:1] if reverse else b_r[:, L - 1:L]
        w_end = g - b_r + i_r
        m_loc = jnp.max(w_end, axis=1, keepdims=True)
        e_end = jnp.exp(w_end - m_loc)
        m_prev = m_ref[h][:, 0:1]
        m_new = jnp.maximum(g + m_prev, m_loc)
        a = jnp.exp(g + m_prev - m_new)
        bl = jnp.exp(m_loc - m_new)
        hs = slice(h * LANES, (h + 1) * LANES)
        kt_h = kt_ref[hs, :]
        q_h = q_ref[:, hs]
        v_h = v_ref[:, hs]
        aug = jnp.concatenate([v_h, ones_aug], axis=1)
        c_loc = _dot((kt_h.astype(F32) * e_end).astype(BF16), aug)
        st = st_ref[h]
        q_st = _dot(q_h, st.astype(BF16))
        skq = _dot(q_h, kt_h)
        dm = jnp.where(causal, b_c - b_r + i_r, -jnp.inf)
        a_t = b_c + m_prev
        m_t = jnp.maximum(a_t, jnp.max(dm, axis=1, keepdims=True))
        w = jnp.exp(dm - m_t) * skq
        inter = jnp.exp(a_t - m_t)
        numer = inter * q_st[:, 0:LANES] + _dot(w.astype(BF16), v_h)
        denom = inter * q_st[:, LANES:2 * LANES] + jnp.sum(w, axis=1, keepdims=True)
        o_ref[:, hs] = numer / jnp.maximum(jnp.abs(denom), jnp.exp(-m_t))
        st_ref[h] = a * st + bl * c_loc
        m_ref[h] = jnp.broadcast_to(m_new, (1, LANES))


def _ml_scan(q, kt, v, gc, gr, batch, seq, reverse):
    T = q.shape[0]
    nc = seq // ML_CHUNK

    def chunk(b, c):
        cc = nc - 1 - c if reverse else c
        return b * nc + cc

    kern = functools.partial(_mlscan_kernel, reverse=reverse)
    return pl.pallas_call(
        kern,
        out_shape=jax.ShapeDtypeStruct((T, 512), F32),
        grid=(batch, nc),
        in_specs=[
            pl.BlockSpec((ML_CHUNK, 512), lambda b, c: (chunk(b, c), 0)),
            pl.BlockSpec((512, ML_CHUNK), lambda b, c: (0, chunk(b, c))),
            pl.BlockSpec((ML_CHUNK, 512), lambda b, c: (chunk(b, c), 0)),
            pl.BlockSpec((ML_CHUNK, 16), lambda b, c: (chunk(b, c), 0)),
            pl.BlockSpec((16, ML_CHUNK), lambda b, c: (0, chunk(b, c))),
        ],
        out_specs=pl.BlockSpec((ML_CHUNK, 512), lambda b, c: (chunk(b, c), 0)),
        scratch_shapes=[
            pltpu.VMEM((ML_HEADS, ML_HEAD_DIM, 2 * LANES), F32),
            pltpu.VMEM((ML_HEADS, 1, LANES), F32),
        ],
        compiler_params=_cparams(("parallel", "arbitrary")),
        name="mlstm_scan_bwd" if reverse else "mlstm_scan_fwd",
    )(q, kt, v, gc, gr)


def _layer_norm(x, g, b):
    mu = jnp.mean(x, axis=1, keepdims=True)
    xc = x - mu
    var = jnp.mean(xc * xc, axis=1, keepdims=True)
    return xc * lax.rsqrt(var + EPS) * g + b


def _merge_kernel(x_ref, ya_ref, hf_ref, hb_ref, xc_ref, op_ref, yc_ref, g_ref,
                  nw_ref, sk_ref, wa_ref, wb_ref, wc_ref, wo_ref, lg_ref, lb_ref, rw_ref, rb_ref,
                  x1_ref, tw_ref, rk_ref, cnt_ref, *, tm):
    hsum = hf_ref[...] + hb_ref[...]
    parts = []
    for h in range(ML_HEADS):
        hh = hsum[:, h * LANES:(h + 1) * LANES]
        mu = jnp.mean(hh, axis=1, keepdims=True)
        d = hh - mu
        var = jnp.mean(d * d, axis=1, keepdims=True)
        parts.append(d * lax.rsqrt(var + EPS))
    hn = jnp.concatenate(parts, axis=1) * nw_ref[...]
    hn = hn + sk_ref[...] * xc_ref[...].astype(F32)
    yb = (jax.nn.sigmoid(op_ref[...].astype(F32)) * hn).astype(BF16)

    g = g_ref[...].astype(F32)
    merged = (jax.nn.sigmoid(g[:, 0:D_MODEL]) * _dot(ya_ref[...], wa_ref[...])
              + jax.nn.sigmoid(g[:, D_MODEL:2 * D_MODEL]) * _dot(yb, wb_ref[...])
              + jax.nn.sigmoid(g[:, 2 * D_MODEL:3 * D_MODEL]) * _dot(yc_ref[...], wc_ref[...]))
    y = _dot(merged.astype(BF16), wo_ref[...])
    x1 = _layer_norm(DN_ALPHA * x_ref[...] + y, lg_ref[...], lb_ref[...])
    x1_ref[...] = x1

    x_hi, x_lo = _split_bf16(x1)
    rw = rw_ref[...]
    w_hi, w_lo = _split_bf16(rw)
    logits = _dot(x_hi, w_hi) + _dot(x_hi, w_lo) + _dot(x_lo, w_hi) + rb_ref[...]

    eidx = lax.broadcasted_iota(jnp.int32, (tm, N_EXPERTS), 1)
    work = logits
    vals, idxs = [], []
    for _ in range(TOP_K):
        mx = jnp.max(work, axis=1, keepdims=True)
        sel = jnp.min(jnp.where(work == mx, eidx, N_EXPERTS), axis=1, keepdims=True)
        vals.append(mx)
        idxs.append(sel)
        work = jnp.where(eidx == sel, -jnp.inf, work)
    tv = jnp.concatenate(vals, axis=1)
    e = jnp.exp(tv - tv[:, 0:1])
    tw_ref[...] = e / jnp.sum(e, axis=1, keepdims=True)

    onehots = [(eidx == idxs[k]) for k in range(TOP_K)]
    osum = jnp.zeros((tm, N_EXPERTS), F32)
    for k in range(TOP_K):
        osum = osum + jnp.where(onehots[k], 1.0, 0.0)
    ri = lax.broadcasted_iota(jnp.int32, (tm, tm), 0)
    ci = lax.broadcasted_iota(jnp.int32, (tm, tm), 1)
    strict = jnp.where(ci < ri, 1.0, 0.0).astype(BF16)
    before = _dot(strict, osum.astype(BF16))
    cnt = jnp.sum(osum, axis=0, keepdims=True)
    seg = jnp.ceil(cnt * (1.0 / SEG_ALIGN)) * SEG_ALIGN
    er = lax.broadcasted_iota(jnp.int32, (N_EXPERTS, N_EXPERTS), 0)
    ec = lax.broadcasted_iota(jnp.int32, (N_EXPERTS, N_EXPERTS), 1)
    upper = jnp.where(er < ec, 1.0, 0.0).astype(BF16)
    seg8 = jnp.broadcast_to(seg, (8, N_EXPERTS)).astype(BF16)
    seg_start = _dot(seg8, upper)[0:1, :]
    local = before + seg_start
    rows = [jnp.sum(jnp.where(onehots[k], local, 0.0), axis=1, keepdims=True) for k in range(TOP_K)]
    rk_ref[...] = jnp.concatenate(rows, axis=1).astype(jnp.int32)
    cnt_ref[...] = cnt.astype(jnp.int32)


def _merge(x2, ya, hf, hb, xc, p, yc, norm_w, skip, wa, wb, wc, wo, ln_g, ln_b, rw, rb, layer):
    T = x2.shape[0]
    tm = _route_tile(T)
    kern = functools.partial(_merge_kernel, tm=tm)
    row = lambda i: (i, 0)
    cw = lambda i: (layer, 0, 0)
    return pl.pallas_call(
        kern,
        out_shape=(
            jax.ShapeDtypeStruct((T, D_MODEL), F32),
            jax.ShapeDtypeStruct((T, TOP_K), F32),
            jax.ShapeDtypeStruct((T, TOP_K), jnp.int32),
            jax.ShapeDtypeStruct((T // tm, 1, N_EXPERTS), jnp.int32),
        ),
        grid=(T // tm,),
        in_specs=[
            pl.BlockSpec((tm, D_MODEL), row),
            pl.BlockSpec((tm, 512), row),
            pl.BlockSpec((tm, 512), row),
            pl.BlockSpec((tm, 512), row),
            pl.BlockSpec((tm, 512), row),
            pl.BlockSpec((tm, 512), lambda i: (i, MLO_OFF // 512)),
            pl.BlockSpec((tm, 512), row),
            pl.BlockSpec((tm, 3 * D_MODEL), lambda i: (i, 0)),
            pl.BlockSpec((None, 1, 512), cw),
            pl.BlockSpec((None, 1, 512), cw),
            pl.BlockSpec((None, 512, D_MODEL), cw),
            pl.BlockSpec((None, 512, D_MODEL), cw),
            pl.BlockSpec((None, 512, D_MODEL), cw),
            pl.BlockSpec((None, D_MODEL, D_MODEL), cw),
            pl.BlockSpec((None, 1, D_MODEL), cw),
            pl.BlockSpec((None, 1, D_MODEL), cw),
            pl.BlockSpec((None, D_MODEL, N_EXPERTS), cw),
            pl.BlockSpec((None, 1, N_EXPERTS), cw),
        ],
        out_specs=(
            pl.BlockSpec((tm, D_MODEL), row),
            pl.BlockSpec((tm, TOP_K), row),
            pl.BlockSpec((tm, TOP_K), row),
            pl.BlockSpec((None, 1, N_EXPERTS), lambda i: (i, 0, 0)),
        ),
        compiler_params=_cparams(("parallel",)),
        name="merge_router",
    )(x2, ya, hf, hb, xc, p, yc, p, norm_w, skip, wa, wb, wc, wo, ln_g, ln_b, rw, rb)


SEG_ALIGN = 8
TAB_DST, TAB_LEN, TAB_SRC = 0, N_EXPERTS, 2 * N_EXPERTS
TAIL_DST, TAIL_LEN, TAIL_USED = 0, N_EXPERTS, 2 * N_EXPERTS


def _route_tile(T):
    return min(512, T)


def _seg_rows(tm):
    return tm * TOP_K + N_EXPERTS * SEG_ALIGN


def _segment(tab_ref, e):
    n = pl.multiple_of(tab_ref[0, 0, TAB_LEN + e], SEG_ALIGN)
    src = pl.multiple_of(tab_ref[0, 0, TAB_SRC + e], SEG_ALIGN)
    dst = pl.multiple_of(tab_ref[0, 0, TAB_DST + e], SEG_ALIGN)
    return n, src, dst


def _pair_matrix(lp, values, nb):
    col = lax.broadcasted_iota(jnp.int32, (1, nb), 1)
    m = jnp.zeros((lp.shape[0], nb), F32)
    for k in range(TOP_K):
        m = m + jnp.where(lp[:, k:k + 1] == col, values[k], 0.0)
    return m


def _scatter_kernel(tab_ref, tail_ref, lp_ref, x_ref, o_ref, buf_ref, zero_ref, sem, *, nb, moe_tm, n_tiles):
    i = pl.program_id(0)
    sel = _pair_matrix(lp_ref[...], [1.0] * TOP_K, nb).astype(BF16)
    buf_ref[...] = lax.dot_general(sel, x_ref[...].astype(BF16), (((0,), (0,)), ((), ())),
                                   preferred_element_type=F32)

    def seg_copy(e):
        n, src, dst = _segment(tab_ref, e)
        return n, pltpu.make_async_copy(buf_ref.at[pl.ds(src, n), :], o_ref.at[pl.ds(dst, n), :], sem)

    for e in range(N_EXPERTS):
        n, cp = seg_copy(e)
        pl.when(n > 0)(cp.start)
    for e in range(N_EXPERTS):
        n, cp = seg_copy(e)
        pl.when(n > 0)(cp.wait)

    @pl.when(i == pl.num_programs(0) - 1)
    def _():
        zero_ref[...] = jnp.zeros(zero_ref.shape, F32)

        def tail_copy(e):
            n = pl.multiple_of(tail_ref[0, 0, TAIL_LEN + e], SEG_ALIGN)
            dst = pl.multiple_of(tail_ref[0, 0, TAIL_DST + e], SEG_ALIGN)
            return n, pltpu.make_async_copy(zero_ref.at[pl.ds(0, n), :], o_ref.at[pl.ds(dst, n), :], sem)

        for e in range(N_EXPERTS):
            n, cp = tail_copy(e)
            pl.when(n > 0)(cp.start)
        for e in range(N_EXPERTS):
            n, cp = tail_copy(e)
            pl.when(n > 0)(cp.wait)

        def tile_copy(j):
            dst = pl.multiple_of(j * moe_tm, moe_tm)
            return pltpu.make_async_copy(zero_ref, o_ref.at[pl.ds(dst, moe_tm), :], sem)

        def fill(j, carry):
            cp = tile_copy(j)
            cp.start()
            cp.wait()
            return carry

        lax.fori_loop(tail_ref[0, 0, TAIL_USED], n_tiles, fill, 0)


def _scatter_rows(tab, tail, lpos, x1, rows_pad, moe_tm):
    T = x1.shape[0]
    tm = _route_tile(T)
    nb = _seg_rows(tm)
    kern = functools.partial(_scatter_kernel, nb=nb, moe_tm=moe_tm, n_tiles=rows_pad // moe_tm)
    return pl.pallas_call(
        kern,
        out_shape=jax.ShapeDtypeStruct((rows_pad, D_MODEL), F32),
        grid=(T // tm,),
        in_specs=[
            pl.BlockSpec((1, 1, 4 * N_EXPERTS), lambda i: (i, 0, 0), memory_space=pltpu.SMEM),
            pl.BlockSpec((1, 1, 4 * N_EXPERTS), lambda i: (0, 0, 0), memory_space=pltpu.SMEM),
            pl.BlockSpec((tm, TOP_K), lambda i: (i, 0)),
            pl.BlockSpec((tm, D_MODEL), lambda i: (i, 0)),
        ],
        out_specs=pl.BlockSpec(memory_space=pl.ANY),
        scratch_shapes=[pltpu.VMEM((nb, D_MODEL), F32), pltpu.VMEM((moe_tm, D_MODEL), F32),
                        pltpu.SemaphoreType.DMA(())],
        compiler_params=_cparams(("arbitrary",)),
        name="moe_scatter",
    )(tab, tail, lpos, x1)


GU_GROUP = 2 * LANES


def _regroup_kernel(w_ref, o_ref):
    ri = lax.broadcasted_iota(jnp.int32, (GU_GROUP, GU_GROUP), 0)
    ci = lax.broadcasted_iota(jnp.int32, (GU_GROUP, GU_GROUP), 1)
    src = jnp.where(ci < LANES, 2 * ci, 2 * (ci - LANES) + 1)
    perm = jnp.where(ri == src, 1.0, 0.0).astype(BF16)
    for b in range(2 * D_FF // GU_GROUP):
        cols = slice(b * GU_GROUP, (b + 1) * GU_GROUP)
        o_ref[:, cols] = _dot(w_ref[:, cols].astype(BF16), perm).astype(BF16)


def _regroup_gate_up(w_gu):
    L, E = w_gu.shape[0], w_gu.shape[1]
    tr = 512
    return pl.pallas_call(
        _regroup_kernel,
        out_shape=jax.ShapeDtypeStruct((L, E, D_MODEL, 2 * D_FF), BF16),
        grid=(L, E, D_MODEL // tr),
        in_specs=[pl.BlockSpec((None, None, tr, 2 * D_FF), lambda l, e, r: (l, e, r, 0))],
        out_specs=pl.BlockSpec((None, None, tr, 2 * D_FF), lambda l, e, r: (l, e, r, 0)),
        compiler_params=_cparams(("parallel", "parallel", "parallel")),
        name="regroup_gate_up",
    )(w_gu)


def _regroup_bias(b_gu):
    lead = b_gu.shape[:-1]
    g = b_gu.reshape(lead + (2 * D_FF // GU_GROUP, LANES, 2))
    return jnp.swapaxes(g, -1, -2).reshape(lead + (2 * D_FF,))


def _moe_kernel(te_ref, tr_ref, nt_ref, xs_ref, wgu_ref, bgu_ref, wd_ref, bd_ref, o_ref):
    i = pl.program_id(0)

    @pl.when(i < nt_ref[0])
    def _():
        h = _dot(xs_ref[...].astype(BF16), wgu_ref[...]) + bgu_ref[...]
        acts = []
        for b in range(D_FF // LANES):
            gate = jnp.minimum(h[:, 2 * b * LANES:(2 * b + 1) * LANES], SWIGLU_LIMIT)
            up = jnp.clip(h[:, (2 * b + 1) * LANES:(2 * b + 2) * LANES], -SWIGLU_LIMIT, SWIGLU_LIMIT)
            acts.append((gate * jax.nn.sigmoid(SWIGLU_ALPHA * gate) * (up + 1.0)).astype(BF16))
        act = jnp.concatenate(acts, axis=1)
        y = _dot(act, wd_ref[...].astype(BF16)) + bd_ref[...]
        o_ref[...] = y

    @pl.when(i >= nt_ref[0])
    def _():
        o_ref[...] = jnp.zeros(o_ref.shape, F32)


def _moe_experts(tile_expert, tile_row, n_tiles_used, xs, wgu, bgu, wd, bd, layer, tm):
    rows_pad = xs.shape[0]
    n_tiles = rows_pad // tm
    return pl.pallas_call(
        _moe_kernel,
        out_shape=jax.ShapeDtypeStruct((rows_pad, D_MODEL), F32),
        grid_spec=pltpu.PrefetchScalarGridSpec(
            num_scalar_prefetch=3,
            grid=(n_tiles,),
            in_specs=[
                pl.BlockSpec((tm, D_MODEL), lambda i, te, tr, nt: (tr[i], 0)),
                pl.BlockSpec((None, None, D_MODEL, 2 * D_FF), lambda i, te, tr, nt: (layer, te[i], 0, 0)),
                pl.BlockSpec((None, None, 1, 2 * D_FF), lambda i, te, tr, nt: (layer, te[i], 0, 0)),
                pl.BlockSpec((None, None, D_FF, D_MODEL), lambda i, te, tr, nt: (layer, te[i], 0, 0)),
                pl.BlockSpec((None, None, 1, D_MODEL), lambda i, te, tr, nt: (layer, te[i], 0, 0)),
            ],
            out_specs=pl.BlockSpec((tm, D_MODEL), lambda i, te, tr, nt: (i, 0)),
        ),
        compiler_params=_cparams(("arbitrary",)),
        name="moe_experts",
    )(tile_expert, tile_row, n_tiles_used, xs, wgu, bgu, wd, bd)


def _combine_kernel(tab_ref, lp_ref, x_ref, tw_ref, lg_ref, lb_ref, ys_ref, o_ref, buf_ref, sem, *, nb):
    i = pl.program_id(0)

    @pl.when(i == 0)
    def _():
        buf_ref[...] = jnp.zeros(buf_ref.shape, F32)

    def seg_copy(e):
        n, dst, src = _segment(tab_ref, e)
        return n, pltpu.make_async_copy(ys_ref.at[pl.ds(src, n), :], buf_ref.at[pl.ds(dst, n), :], sem)

    for e in range(N_EXPERTS):
        n, cp = seg_copy(e)
        pl.when(n > 0)(cp.start)
    for e in range(N_EXPERTS):
        n, cp = seg_copy(e)
        pl.when(n > 0)(cp.wait)

    tw = tw_ref[...]
    wsel = _pair_matrix(lp_ref[...], [tw[:, k:k + 1] for k in range(TOP_K)], nb)
    y = _dot(wsel.astype(BF16), buf_ref[...].astype(BF16))
    o_ref[...] = _layer_norm(DN_ALPHA * x_ref[...] + y, lg_ref[...], lb_ref[...])


def _combine(tab, lpos, x1, tw, ln_g, ln_b, ys, layer):
    T = x1.shape[0]
    tm = _route_tile(T)
    nb = _seg_rows(tm)
    kern = functools.partial(_combine_kernel, nb=nb)
    return pl.pallas_call(
        kern,
        out_shape=jax.ShapeDtypeStruct((T, D_MODEL), F32),
        grid=(T // tm,),
        in_specs=[
            pl.BlockSpec((1, 1, 4 * N_EXPERTS), lambda i: (i, 0, 0), memory_space=pltpu.SMEM),
            pl.BlockSpec((tm, TOP_K), lambda i: (i, 0)),
            pl.BlockSpec((tm, D_MODEL), lambda i: (i, 0)),
            pl.BlockSpec((tm, TOP_K), lambda i: (i, 0)),
            pl.BlockSpec((None, 1, D_MODEL), lambda i: (layer, 0, 0)),
            pl.BlockSpec((None, 1, D_MODEL), lambda i: (layer, 0, 0)),
            pl.BlockSpec(memory_space=pl.ANY),
        ],
        out_specs=pl.BlockSpec((tm, D_MODEL), lambda i: (i, 0)),
        scratch_shapes=[pltpu.VMEM((nb, D_MODEL), F32), pltpu.SemaphoreType.DMA(())],
        compiler_params=_cparams(("arbitrary",)),
        name="moe_combine",
    )(tab, lpos, x1, tw, ln_g, ln_b, ys)


def _block_diag(w):
    L, H, d, e = w.shape
    eye = jnp.eye(H, dtype=w.dtype)
    return jnp.einsum('lhde,hg->lhdge', w, eye).reshape(L, H * d, H * e)


def _rope_tables(seq):
    pos = jnp.arange(seq, dtype=F32)
    inv = ROPE_THETA ** (-jnp.arange(0, HEAD_DIM, 2, dtype=F32) / HEAD_DIM)
    ang = pos[:, None] * inv[None, :]
    cos = jnp.tile(jnp.cos(ang), (1, 4))
    sin = jnp.tile(jnp.sin(ang), (1, 4))
    sign = jnp.where(jnp.arange(LANES) < 64, -1.0, 1.0).astype(F32)
    return cos, sin * sign[None, :]


def _routing_tables(counts, tm, rows_pad):
    cnt = counts[:, 0, :]
    nt = cnt.shape[0]
    seg = ((cnt + SEG_ALIGN - 1) // SEG_ALIGN) * SEG_ALIGN
    local = jnp.cumsum(seg, axis=1) - seg
    gsize = jnp.sum(seg, axis=0)
    gpad = ((gsize + tm - 1) // tm) * tm
    ends = jnp.cumsum(gpad)
    offs = ends - gpad
    dst = offs[None, :] + jnp.cumsum(seg, axis=0) - seg
    zeros = jnp.zeros_like(seg)
    tab = jnp.concatenate([dst, seg, local, zeros], axis=1).astype(jnp.int32).reshape(nt, 1, 4 * N_EXPERTS)
    n_tiles = rows_pad // tm
    used = (ends[-1] // tm).astype(jnp.int32)
    tail = jnp.concatenate([offs + gsize, gpad - gsize, jnp.broadcast_to(used, (N_EXPERTS,)),
                            jnp.zeros((N_EXPERTS,), jnp.int32)]).astype(jnp.int32).reshape(1, 1, 4 * N_EXPERTS)
    starts = jnp.arange(n_tiles, dtype=jnp.int32) * tm
    te = jnp.sum((starts[:, None] >= ends[None, :]).astype(jnp.int32), axis=1)
    last = jnp.maximum(used - 1, 0)
    te_last = te[last]
    valid = jnp.arange(n_tiles, dtype=jnp.int32) < used
    tile_expert = jnp.where(valid, te, te_last).astype(jnp.int32)
    tile_row = jnp.where(valid, jnp.arange(n_tiles, dtype=jnp.int32), last).astype(jnp.int32)
    return tab, tail, tile_expert, tile_row, used.reshape(1)


def kernel(x, w_in, da_lambda, da_subln, w_proj_a, ml_conv_w, ml_conv_b, ml_wq, ml_wk, ml_wv,
           ml_w_gates, ml_b_gates, ml_norm, ml_skip, w_proj_b, wa_sinks, w_proj_c, w_out,
           ln1_g, ln1_b, router_w, router_b, exp_w_gu, exp_b_gu, exp_w_down, exp_b_down,
           ln2_g, ln2_b):
    B, S, _ = x.shape
    T = B * S
    L = w_in.shape[0]
    moe_tm = min(512, T)
    route_tiles = T // _route_tile(T)
    rows_max = T * TOP_K + route_tiles * N_EXPERTS * SEG_ALIGN + N_EXPERTS * moe_tm
    rows_pad = ((rows_max + moe_tm - 1) // moe_tm) * moe_tm

    col_idx, col_scale = _proj_columns()
    w_all = (w_in[:, :, col_idx] * col_scale[None, None, :]).astype(BF16)
    cos_t, sin_t = _rope_tables(S)
    wqk = jnp.concatenate([_block_diag(ml_wq), _block_diag(ml_wk)], axis=2).astype(BF16)
    wv = _block_diag(ml_wv).astype(BF16)
    wkt = jnp.swapaxes(_block_diag(ml_wk), 1, 2).astype(BF16)
    wg = ml_w_gates.astype(BF16)
    wgt = jnp.swapaxes(ml_w_gates, 1, 2).astype(BF16)
    bg = ml_b_gates[:, None, :]
    bgt = ml_b_gates[:, :, None]
    wgu = _regroup_gate_up(exp_w_gu)
    bgu = _regroup_bias(exp_b_gu)[:, :, None, :]
    bd = exp_b_down[:, :, None, :]
    sub = da_subln[:, None, :]
    wa_b, wb_b, wc_b, wo_b = (w.astype(BF16) for w in (w_proj_a, w_proj_b, w_proj_c, w_out))
    r3 = lambda a: a[:, None, :]

    x2 = x.reshape(T, D_MODEL)
    for l in range(L):
        p = _inproj(x2, w_all, l, cos_t, sin_t, S)
        ya = _diff_attention(p, da_lambda, sub, l, B, S)
        yc = _window_attention(p, r3(wa_sinks), l, B, S)
        q, kt, v, xc, gc, gr = _ml_prep(p, ml_conv_w, r3(ml_conv_b), wqk, wv, wkt, wg, wgt, bg, bgt, l, S)
        hf = _ml_scan(q, kt, v, gc, gr, B, S, reverse=False)
        hb = _ml_scan(q, kt, v, gc, gr, B, S, reverse=True)
        x1, top_w, lpos, counts = _merge(
            x2, ya, hf, hb, xc, p, yc, r3(ml_norm), r3(ml_skip), wa_b, wb_b, wc_b, wo_b,
            r3(ln1_g), r3(ln1_b), router_w, r3(router_b), l)
        tab, tail, tile_expert, tile_row, used = _routing_tables(counts, moe_tm, rows_pad)
        xs = _scatter_rows(tab, tail, lpos, x1, rows_pad, moe_tm)
        ys = _moe_experts(tile_expert, tile_row, used, xs, wgu, bgu, exp_w_down, bd, l, moe_tm)
        x2 = _combine(tab, lpos, x1, top_w, r3(ln2_g), r3(ln2_b), ys, l)
    return x2.reshape(B, S, D_MODEL)
```

```python
import functools
import math

import jax
import jax.numpy as jnp
import numpy as np
from jax import lax
from jax.experimental import pallas as pl
from jax.experimental.pallas import tpu as pltpu

F32 = jnp.float32
BF16 = jnp.bfloat16

D_MODEL = 1024
DEPTH = 4
DA_HEADS = 4
HEAD_DIM = 64
ML_HEADS = 4
ML_HEAD_DIM = 128
ML_CONV = 5
ML_CHUNK = 128
WA_Q_HEADS = 8
WA_KV_HEADS = 2
WA_BLOCK = 128
N_EXPERTS = 32
TOP_K = 4
D_FF = D_MODEL
SWIGLU_LIMIT = 7.0
SWIGLU_ALPHA = 1.702
ROPE_THETA = 10000.0
DN_ALPHA = (2.0 * DEPTH) ** 0.25
EPS = 1e-5

LANES = 128
VMEM_LIMIT = 56 * 1024 * 1024

GATE_OFF = 0
QA_OFF = 3072
KA_OFF = 3584
QC_OFF = 4096
KCD_OFF = 4608
VCD_OFF = 4864
VA_OFF = 5120
MLX_OFF = 5632
MLO_OFF = 6144
P_COLS = 6656
PROJ_TN = 512
ROPE_FULL_BLOCKS = (QA_OFF // PROJ_TN, KA_OFF // PROJ_TN, QC_OFF // PROJ_TN)
ROPE_HALF_BLOCK = KCD_OFF // PROJ_TN

_R_DA_Q, _R_DA_K, _R_DA_V = 0, 512, 1024
_R_ML_X, _R_ML_O = 1536, 2048
_R_WA_Q, _R_WA_K, _R_WA_V, _R_GATES = 2560, 3072, 3200, 3328


def _pair_perm():
    return np.concatenate([np.arange(0, 32), np.arange(64, 96), np.arange(32, 64), np.arange(96, 128)])


def _proj_columns():
    perm = _pair_perm()
    idx = np.zeros((P_COLS,), np.int32)
    scale = np.ones((P_COLS,), np.float32)
    idx[GATE_OFF:GATE_OFF + 3072] = _R_GATES + np.arange(3072)
    for t in range(4):
        idx[QA_OFF + t * 128:QA_OFF + (t + 1) * 128] = _R_DA_Q + t * 128 + perm
        idx[KA_OFF + t * 128:KA_OFF + (t + 1) * 128] = _R_DA_K + t * 128 + perm
        idx[QC_OFF + t * 128:QC_OFF + (t + 1) * 128] = _R_WA_Q + t * 128 + perm
    scale[QA_OFF:QA_OFF + 512] = HEAD_DIM ** -0.5
    scale[QC_OFF:QC_OFF + 512] = HEAD_DIM ** -0.5
    dup = np.concatenate([np.arange(64), np.arange(64)])
    for g in range(2):
        idx[KCD_OFF + g * 128:KCD_OFF + (g + 1) * 128] = _R_WA_K + g * 64 + dup[perm]
        idx[VCD_OFF + g * 128:VCD_OFF + (g + 1) * 128] = _R_WA_V + g * 64 + dup
    idx[VA_OFF:VA_OFF + 512] = _R_DA_V + np.arange(512)
    idx[MLX_OFF:MLX_OFF + 512] = _R_ML_X + np.arange(512)
    idx[MLO_OFF:MLO_OFF + 512] = _R_ML_O + np.arange(512)
    return idx, scale


def _cparams(sem, vmem=VMEM_LIMIT):
    return pltpu.CompilerParams(dimension_semantics=sem, vmem_limit_bytes=vmem)


def _dot(a, b):
    return jnp.dot(a, b, preferred_element_type=F32)


def _dot_nt(a, b):
    return lax.dot_general(a, b, (((1,), (1,)), ((), ())), preferred_element_type=F32)


def _lane_repeat(x, reps):
    return jnp.concatenate([x] * reps, axis=1)


def _split_bf16(x):
    hi = x.astype(BF16)
    lo = (x - hi.astype(F32)).astype(BF16)
    return hi, lo


def _inproj_kernel(x_ref, w_ref, c_ref, s_ref, o_ref, xb_ref):
    j = pl.program_id(1)

    @pl.when(j == 0)
    def _():
        xb_ref[...] = x_ref[...].astype(BF16)

    def rope(a):
        return a * c_ref[...] + pltpu.roll(a, 64, 1) * s_ref[...]

    is_full = (j == ROPE_FULL_BLOCKS[0]) | (j == ROPE_FULL_BLOCKS[1]) | (j == ROPE_FULL_BLOCKS[2])
    is_half = j == ROPE_HALF_BLOCK

    def roped_store(n_rope):
        acc = _dot(xb_ref[...], w_ref[...])
        for t in range(PROJ_TN // LANES):
            a = acc[:, t * LANES:(t + 1) * LANES]
            o_ref[:, t * LANES:(t + 1) * LANES] = (rope(a) if t < n_rope else a).astype(BF16)

    @pl.when(is_full)
    def _():
        roped_store(4)

    @pl.when(is_half)
    def _():
        roped_store(2)

    @pl.when(jnp.logical_not(is_full | is_half))
    def _():
        o_ref[...] = _dot(xb_ref[...], w_ref[...]).astype(BF16)


def _inproj(x2, w_all, layer, cos_t, sin_t, seq):
    T = x2.shape[0]
    tm = min(2048, seq)
    ns = seq // tm
    return pl.pallas_call(
        _inproj_kernel,
        out_shape=jax.ShapeDtypeStruct((T, P_COLS), BF16),
        grid=(T // tm, P_COLS // PROJ_TN),
        in_specs=[
            pl.BlockSpec((tm, D_MODEL), lambda i, j: (i, 0)),
            pl.BlockSpec((None, D_MODEL, PROJ_TN), lambda i, j: (layer, 0, j)),
            pl.BlockSpec((tm, LANES), lambda i, j: (i % ns, 0)),
            pl.BlockSpec((tm, LANES), lambda i, j: (i % ns, 0)),
        ],
        out_specs=pl.BlockSpec((tm, PROJ_TN), lambda i, j: (i, j)),
        scratch_shapes=[pltpu.VMEM((tm, D_MODEL), BF16)],
        compiler_params=_cparams(("parallel", "arbitrary")),
        name="inproj",
    )(x2, w_all, cos_t, sin_t)


def _da_kernel(lp_ref, q_ref, k_ref, v_ref, sub_ref, o_ref, qs_ref, m_ref, l_ref, acc_ref,
               *, tq, tk, lam_init):
    seq = k_ref.shape[0]
    lane = lax.broadcasted_iota(jnp.int32, (1, LANES), 1)
    map0 = (lane % 64) < 32
    q = q_ref[...].astype(F32)
    qs_ref[0:tq, :] = jnp.where(map0, q, 0.0).astype(BF16)
    qs_ref[tq:2 * tq, :] = jnp.where(map0, 0.0, q).astype(BF16)
    m_ref[...] = jnp.full(m_ref.shape, -jnp.inf, F32)
    l_ref[...] = jnp.zeros(l_ref.shape, F32)
    acc_ref[...] = jnp.zeros(acc_ref.shape, F32)
    reps = tk // LANES

    def body(c, carry):
        start = pl.multiple_of(c * tk, tk)
        s = _dot_nt(qs_ref[...], k_ref[pl.ds(start, tk), :])
        m_old = m_ref[...]
        m_new = jnp.maximum(m_old, jnp.max(s, axis=1, keepdims=True))
        alpha = jnp.exp(m_old - m_new)
        p = jnp.exp(s - _lane_repeat(m_new, reps))
        l_ref[...] = alpha * l_ref[...] + jnp.sum(p, axis=1, keepdims=True)
        acc_ref[...] = alpha * acc_ref[...] + _dot(p.astype(BF16), v_ref[pl.ds(start, tk), :])
        m_ref[...] = m_new
        return carry

    lax.fori_loop(0, seq // tk, body, 0, unroll=4)

    lp = lp_ref[...]
    l01 = jnp.sum(lp[0:1, :] * lp[1:2, :], axis=1, keepdims=True)
    l23 = jnp.sum(lp[2:3, :] * lp[3:4, :], axis=1, keepdims=True)
    lam = jnp.exp(l01) - jnp.exp(l23) + lam_init
    o = acc_ref[...] / l_ref[...]
    a = o[0:tq, :] - lam * o[tq:2 * tq, :]
    ms = jnp.mean(a * a, axis=1, keepdims=True)
    a = a * lax.rsqrt(ms + EPS) * sub_ref[...] * (1.0 - lam_init)
    o_ref[...] = a.astype(BF16)


def _diff_attention(p, lam_params, subln, layer, batch, seq):
    T = p.shape[0]
    tq = min(512, seq)
    tk = min(1024, seq)
    nq = seq // tq
    lam_init = 0.8 - 0.6 * math.exp(-0.3 * layer)
    kern = functools.partial(_da_kernel, tq=tq, tk=tk, lam_init=lam_init)
    return pl.pallas_call(
        kern,
        out_shape=jax.ShapeDtypeStruct((T, DA_HEADS * LANES), BF16),
        grid=(batch, DA_HEADS, nq),
        in_specs=[
            pl.BlockSpec((None, 4, HEAD_DIM), lambda b, h, i: (layer, 0, 0)),
            pl.BlockSpec((tq, LANES), lambda b, h, i: (b * nq + i, QA_OFF // LANES + h)),
            pl.BlockSpec((seq, LANES), lambda b, h, i: (b, KA_OFF // LANES + h)),
            pl.BlockSpec((seq, LANES), lambda b, h, i: (b, VA_OFF // LANES + h)),
            pl.BlockSpec((None, 1, LANES), lambda b, h, i: (layer, 0, 0)),
        ],
        out_specs=pl.BlockSpec((tq, LANES), lambda b, h, i: (b * nq + i, h)),
        scratch_shapes=[
            pltpu.VMEM((2 * tq, LANES), BF16),
            pltpu.VMEM((2 * tq, LANES), F32),
            pltpu.VMEM((2 * tq, LANES), F32),
            pltpu.VMEM((2 * tq, LANES), F32),
        ],
        compiler_params=_cparams(("parallel", "parallel", "arbitrary")),
        name="diff_attn",
    )(lam_params, p, p, p, subln)


def _wa_kernel(sk_ref, q_ref, kp_ref, kc_ref, kn_ref, vp_ref, vc_ref, vn_ref, o_ref, *, nb):
    n = pl.program_id(1)
    blk = WA_BLOCK
    lane = lax.broadcasted_iota(jnp.int32, (1, LANES), 1)
    map0 = (lane % 64) < 32
    qi = lax.broadcasted_iota(jnp.int32, (blk, 3 * blk), 0)
    ki = lax.broadcasted_iota(jnp.int32, (blk, 3 * blk), 1)
    valid = jnp.abs(qi + blk - ki) <= blk
    valid = valid & jnp.logical_not((n == 0) & (ki < blk))
    valid = valid & jnp.logical_not((n == nb - 1) & (ki >= 2 * blk))
    valid4 = jnp.concatenate([valid] * 4, axis=0)
    sk = sk_ref[...]
    for g in range(WA_KV_HEADS):
        gs = slice(g * LANES, (g + 1) * LANES)
        kg = jnp.concatenate([kp_ref[:, gs], kc_ref[:, gs], kn_ref[:, gs]], axis=0)
        vg = jnp.concatenate([vp_ref[:, gs], vc_ref[:, gs], vn_ref[:, gs]], axis=0)
        t0 = q_ref[:, (2 * g) * LANES:(2 * g + 1) * LANES].astype(F32)
        t1 = q_ref[:, (2 * g + 1) * LANES:(2 * g + 2) * LANES].astype(F32)
        qs = jnp.concatenate([jnp.where(map0, t0, 0.0), jnp.where(map0, 0.0, t0),
                              jnp.where(map0, t1, 0.0), jnp.where(map0, 0.0, t1)], axis=0).astype(BF16)
        s = _dot_nt(qs, kg)
        s = jnp.where(valid4, s, -jnp.inf)
        sink = jnp.concatenate(
            [jnp.broadcast_to(sk[:, 4 * g + r:4 * g + r + 1], (blk, LANES)) for r in range(4)], axis=0)
        m = jnp.maximum(jnp.max(s, axis=1, keepdims=True), sink)
        e = jnp.exp(s - _lane_repeat(m, 3))
        inv = 1.0 / (jnp.sum(e, axis=1, keepdims=True) + jnp.exp(sink - m))
        pr = e * _lane_repeat(inv, 3)
        o = _dot(pr.astype(BF16), vg)
        left = lane < 64
        o_ref[:, (2 * g) * LANES:(2 * g + 1) * LANES] = jnp.where(
            left, o[0:blk], o[blk:2 * blk]).astype(BF16)
        o_ref[:, (2 * g + 1) * LANES:(2 * g + 2) * LANES] = jnp.where(
            left, o[2 * blk:3 * blk], o[3 * blk:4 * blk]).astype(BF16)


def _window_attention(p, sinks, layer, batch, seq):
    T = p.shape[0]
    nb = seq // WA_BLOCK
    kcol = KCD_OFF // 256
    vcol = VCD_OFF // 256

    def prev(b, n):
        return b * nb + jnp.maximum(n - 1, 0)

    def nxt(b, n):
        return b * nb + jnp.minimum(n + 1, nb - 1)

    kern = functools.partial(_wa_kernel, nb=nb)
    return pl.pallas_call(
        kern,
        out_shape=jax.ShapeDtypeStruct((T, WA_Q_HEADS * HEAD_DIM), BF16),
        grid=(batch, nb),
        in_specs=[
            pl.BlockSpec((None, 1, WA_Q_HEADS), lambda b, n: (layer, 0, 0)),
            pl.BlockSpec((WA_BLOCK, 512), lambda b, n: (b * nb + n, QC_OFF // 512)),
            pl.BlockSpec((WA_BLOCK, 256), lambda b, n: (prev(b, n), kcol)),
            pl.BlockSpec((WA_BLOCK, 256), lambda b, n: (b * nb + n, kcol)),
            pl.BlockSpec((WA_BLOCK, 256), lambda b, n: (nxt(b, n), kcol)),
            pl.BlockSpec((WA_BLOCK, 256), lambda b, n: (prev(b, n), vcol)),
            pl.BlockSpec((WA_BLOCK, 256), lambda b, n: (b * nb + n, vcol)),
            pl.BlockSpec((WA_BLOCK, 256), lambda b, n: (nxt(b, n), vcol)),
        ],
        out_specs=pl.BlockSpec((WA_BLOCK, 512), lambda b, n: (b * nb + n, 0)),
        compiler_params=_cparams(("parallel", "arbitrary")),
        name="window_attn",
    )(sinks, p, p, p, p, p, p, p)


HALO = 16


def _log_sigmoid(x):
    return jnp.minimum(x, 0.0) - jnp.log(1.0 + jnp.exp(-jnp.abs(x)))


def _mlprep_kernel(xp_ref, xc_ref, xn_ref, cw_ref, cb_ref, wqk_ref, wv_ref, wkt_ref, wg_ref, wgt_ref,
                   bg_ref, bgt_ref, q_ref, kt_ref, v_ref, xco_ref, gc_ref, gr_ref, *, tm, ns):
    i = pl.program_id(0)
    first = (i % ns) == 0
    last = (i % ns) == ns - 1
    xm = xc_ref[...].astype(F32)
    xp = jnp.where(first, 0.0, xp_ref[...].astype(F32))
    xn = jnp.where(last, 0.0, xn_ref[...].astype(F32))
    ext = jnp.concatenate([xp, xm, xn], axis=0)
    cw = cw_ref[...]
    conv = cb_ref[...]
    for j in range(ML_CONV):
        off = HALO + j - ML_CONV // 2
        conv = conv + ext[off:off + tm, :] * cw[j:j + 1, :]
    xc = conv * jax.nn.sigmoid(conv)
    xcb = xc.astype(BF16)
    qk = _dot(xcb, wqk_ref[...])
    v = _dot(xc_ref[...], wv_ref[...])
    kt = _dot_nt(wkt_ref[...], xcb)
    gin = jnp.concatenate([qk, v], axis=1).astype(BF16)
    gcol = _dot(gin, wg_ref[...]) + bg_ref[...]
    grow = _dot_nt(wgt_ref[...], gin) + bgt_ref[...]
    chc = lax.broadcasted_iota(jnp.int32, (1, 16), 1)
    chr_ = lax.broadcasted_iota(jnp.int32, (16, 1), 0)
    gcol = jnp.where((chc // 4) % 2 == 1, _log_sigmoid(gcol), gcol)
    grow = jnp.where((chr_ // 4) % 2 == 1, _log_sigmoid(grow), grow)
    q_ref[...] = qk[:, 0:512].astype(BF16)
    kt_ref[...] = (kt * (ML_HEAD_DIM ** -0.5)).astype(BF16)
    v_ref[...] = v.astype(BF16)
    xco_ref[...] = xcb
    gc_ref[...] = gcol
    gr_ref[...] = grow


def _ml_prep(p, conv_w, conv_b, wqk, wv, wkt, wg, wgt, bg, bgt, layer, seq):
    T = p.shape[0]
    tm = min(512, seq)
    ns = seq // tm
    r = tm // HALO
    nh = T // HALO
    kern = functools.partial(_mlprep_kernel, tm=tm, ns=ns)
    mcol = MLX_OFF // 512
    const2 = lambda i: (layer, 0, 0)
    return pl.pallas_call(
        kern,
        out_shape=(
            jax.ShapeDtypeStruct((T, 512), BF16),
            jax.ShapeDtypeStruct((512, T), BF16),
            jax.ShapeDtypeStruct((T, 512), BF16),
            jax.ShapeDtypeStruct((T, 512), BF16),
            jax.ShapeDtypeStruct((T, 16), F32),
            jax.ShapeDtypeStruct((16, T), F32),
        ),
        grid=(T // tm,),
        in_specs=[
            pl.BlockSpec((HALO, 512), lambda i: (jnp.maximum(i * r - 1, 0), mcol)),
            pl.BlockSpec((tm, 512), lambda i: (i, mcol)),
            pl.BlockSpec((HALO, 512), lambda i: (jnp.minimum((i + 1) * r, nh - 1), mcol)),
            pl.BlockSpec((None, ML_CONV, 512), const2),
            pl.BlockSpec((None, 1, 512), const2),
            pl.BlockSpec((None, 512, 1024), const2),
            pl.BlockSpec((None, 512, 512), const2),
            pl.BlockSpec((None, 512, 512), const2),
            pl.BlockSpec((None, 1536, 16), const2),
            pl.BlockSpec((None, 16, 1536), const2),
            pl.BlockSpec((None, 1, 16), const2),
            pl.BlockSpec((None, 16, 1), const2),
        ],
        out_specs=(
            pl.BlockSpec((tm, 512), lambda i: (i, 0)),
            pl.BlockSpec((512, tm), lambda i: (0, i)),
            pl.BlockSpec((tm, 512), lambda i: (i, 0)),
            pl.BlockSpec((tm, 512), lambda i: (i, 0)),
            pl.BlockSpec((tm, 16), lambda i: (i, 0)),
            pl.BlockSpec((16, tm), lambda i: (0, i)),
        ),
        compiler_params=_cparams(("parallel",)),
        name="mlstm_prep",
    )(p, p, p, conv_w, conv_b, wqk, wv, wkt, wg, wgt, bg, bgt)


def _mlscan_kernel(qf_ref, ktf_ref, vf_ref, gcf_ref, grf_ref, qb_ref, ktb_ref, vb_ref, gcb_ref, grb_ref,
                   of_ref, ob_ref, stf_ref, mf_ref, stb_ref, mb_ref):
    @pl.when(pl.program_id(1) == 0)
    def _():
        for ref in (stf_ref, mf_ref, stb_ref, mb_ref):
            ref[...] = jnp.zeros(ref.shape, F32)

    _mlscan_dir(qf_ref, ktf_ref, vf_ref, gcf_ref, grf_ref, of_ref, stf_ref, mf_ref, reverse=False)
    _mlscan_dir(qb_ref, ktb_ref, vb_ref, gcb_ref, grb_ref, ob_ref, stb_ref, mb_ref, reverse=True)


def _mlscan_dir(q_ref, kt_ref, v_ref, gc_ref, gr_ref, o_ref, st_ref, m_ref, *, reverse):
    L = ML_CHUNK

    ri = lax.broadcasted_iota(jnp.int32, (L, L), 0)
    ci = lax.broadcasted_iota(jnp.int32, (L, L), 1)
    if reverse:
        row_cum = (ri >= ci)
        col_cum = (ci >= ri)
        causal = ci >= ri
    else:
        row_cum = (ri <= ci)
        col_cum = (ci <= ri)
        causal = ci <= ri
    row_m = jnp.where(row_cum, 1.0, 0.0).astype(BF16)
    col_m = jnp.where(col_cum, 1.0, 0.0).astype(BF16)

    gr = gr_ref[...]
    gc = gc_ref[...]
    gr_hi, gr_lo = _split_bf16(gr)
    gc_hi, gc_lo = _split_bf16(gc)
    brow = _dot(gr_hi, row_m) + _dot(gr_lo, row_m)
    bcol = _dot(col_m, gc_hi) + _dot(col_m, gc_lo)
    ones_aug = jnp.ones((L, LANES), BF16)
    base = 8 if reverse else 0
    for h in range(ML_HEADS):
        ich, fch = base + h, base + 4 + h
        b_r = brow[fch:fch + 1, :]
        i_r = gr[ich:ich + 1, :]
        b_c = bcol[:, fch:fch + 1]
        g = b_r[:, 0:1] if reverse else b_r[:, L - 1:L]
        w_end = g - b_r + i_r
        m_loc = jnp.max(w_end, axis=1, keepdims=True)
        e_end = jnp.exp(w_end - m_loc)
        m_prev = m_ref[h][:, 0:1]
        m_new = jnp.maximum(g + m_prev, m_loc)
        a = jnp.exp(g + m_prev - m_new)
        bl = jnp.exp(m_loc - m_new)
        hs = slice(h * LANES, (h + 1) * LANES)
        kt_h = kt_ref[hs, :]
        q_h = q_ref[:, hs]
        v_h = v_ref[:, hs]
        aug = jnp.concatenate([v_h, ones_aug], axis=1)
        c_loc = _dot((kt_h.astype(F32) * e_end).astype(BF16), aug)
        st = st_ref[h]
        q_st = _dot(q_h, st.astype(BF16))
        skq = _dot(q_h, kt_h)
        dm = jnp.where(causal, b_c - b_r + i_r, -jnp.inf)
        a_t = b_c + m_prev
        m_t = jnp.maximum(a_t, jnp.max(dm, axis=1, keepdims=True))
        w = jnp.exp(dm - m_t) * skq
        inter = jnp.exp(a_t - m_t)
        numer = inter * q_st[:, 0:LANES] + _dot(w.astype(BF16), v_h)
        denom = inter * q_st[:, LANES:2 * LANES] + jnp.sum(w, axis=1, keepdims=True)
        o_ref[:, hs] = numer / jnp.maximum(jnp.abs(denom), jnp.exp(-m_t))
        st_ref[h] = a * st + bl * c_loc
        m_ref[h] = jnp.broadcast_to(m_new, (1, LANES))


def _ml_scan(q, kt, v, gc, gr, batch, seq):
    T = q.shape[0]
    nc = seq // ML_CHUNK

    def specs(chunk):
        return [
            pl.BlockSpec((ML_CHUNK, 512), lambda b, c: (chunk(b, c), 0)),
            pl.BlockSpec((512, ML_CHUNK), lambda b, c: (0, chunk(b, c))),
            pl.BlockSpec((ML_CHUNK, 512), lambda b, c: (chunk(b, c), 0)),
            pl.BlockSpec((ML_CHUNK, 16), lambda b, c: (chunk(b, c), 0)),
            pl.BlockSpec((16, ML_CHUNK), lambda b, c: (0, chunk(b, c))),
        ]

    fwd = lambda b, c: b * nc + c
    bwd = lambda b, c: b * nc + nc - 1 - c
    state = [pltpu.VMEM((ML_HEADS, ML_HEAD_DIM, 2 * LANES), F32), pltpu.VMEM((ML_HEADS, 1, LANES), F32)]
    return pl.pallas_call(
        _mlscan_kernel,
        out_shape=(jax.ShapeDtypeStruct((T, 512), F32), jax.ShapeDtypeStruct((T, 512), F32)),
        grid=(batch, nc),
        in_specs=specs(fwd) + specs(bwd),
        out_specs=(pl.BlockSpec((ML_CHUNK, 512), lambda b, c: (fwd(b, c), 0)),
                   pl.BlockSpec((ML_CHUNK, 512), lambda b, c: (bwd(b, c), 0))),
        scratch_shapes=state + state,
        compiler_params=_cparams(("parallel", "arbitrary")),
        name="mlstm_scan",
    )(q, kt, v, gc, gr, q, kt, v, gc, gr)


def _layer_norm(x, g, b):
    mu = jnp.mean(x, axis=1, keepdims=True)
    xc = x - mu
    var = jnp.mean(xc * xc, axis=1, keepdims=True)
    return xc * lax.rsqrt(var + EPS) * g + b


def _merge_kernel(x_ref, ya_ref, hf_ref, hb_ref, xc_ref, op_ref, yc_ref, g_ref,
                  nw_ref, sk_ref, wa_ref, wb_ref, wc_ref, wo_ref, lg_ref, lb_ref, rw_ref, rb_ref,
                  x1_ref, tw_ref, rk_ref, cnt_ref, *, tm):
    hsum = hf_ref[...] + hb_ref[...]
    parts = []
    for h in range(ML_HEADS):
        hh = hsum[:, h * LANES:(h + 1) * LANES]
        mu = jnp.mean(hh, axis=1, keepdims=True)
        d = hh - mu
        var = jnp.mean(d * d, axis=1, keepdims=True)
        parts.append(d * lax.rsqrt(var + EPS))
    hn = jnp.concatenate(parts, axis=1) * nw_ref[...]
    hn = hn + sk_ref[...] * xc_ref[...].astype(F32)
    yb = (jax.nn.sigmoid(op_ref[...].astype(F32)) * hn).astype(BF16)

    g = g_ref[...].astype(F32)
    merged = (jax.nn.sigmoid(g[:, 0:D_MODEL]) * _dot(ya_ref[...], wa_ref[...])
              + jax.nn.sigmoid(g[:, D_MODEL:2 * D_MODEL]) * _dot(yb, wb_ref[...])
              + jax.nn.sigmoid(g[:, 2 * D_MODEL:3 * D_MODEL]) * _dot(yc_ref[...], wc_ref[...]))
    y = _dot(merged.astype(BF16), wo_ref[...])
    x1 = _layer_norm(DN_ALPHA * x_ref[...] + y, lg_ref[...], lb_ref[...])
    x1_ref[...] = x1

    x_hi, x_lo = _split_bf16(x1)
    rw = rw_ref[...]
    w_hi, w_lo = _split_bf16(rw)
    logits = _dot(x_hi, w_hi) + _dot(x_hi, w_lo) + _dot(x_lo, w_hi) + rb_ref[...]

    eidx = lax.broadcasted_iota(jnp.int32, (tm, N_EXPERTS), 1)
    work = logits
    vals, idxs = [], []
    for _ in range(TOP_K):
        mx = jnp.max(work, axis=1, keepdims=True)
        sel = jnp.min(jnp.where(work == mx, eidx, N_EXPERTS), axis=1, keepdims=True)
        vals.append(mx)
        idxs.append(sel)
        work = jnp.where(eidx == sel, -jnp.inf, work)
    tv = jnp.concatenate(vals, axis=1)
    e = jnp.exp(tv - tv[:, 0:1])
    tw_ref[...] = e / jnp.sum(e, axis=1, keepdims=True)

    onehots = [(eidx == idxs[k]) for k in range(TOP_K)]
    osum = jnp.zeros((tm, N_EXPERTS), F32)
    for k in range(TOP_K):
        osum = osum + jnp.where(onehots[k], 1.0, 0.0)
    ri = lax.broadcasted_iota(jnp.int32, (tm, tm), 0)
    ci = lax.broadcasted_iota(jnp.int32, (tm, tm), 1)
    strict = jnp.where(ci < ri, 1.0, 0.0).astype(BF16)
    before = _dot(strict, osum.astype(BF16))
    cnt = jnp.sum(osum, axis=0, keepdims=True)
    seg = jnp.ceil(cnt * (1.0 / SEG_ALIGN)) * SEG_ALIGN
    er = lax.broadcasted_iota(jnp.int32, (N_EXPERTS, N_EXPERTS), 0)
    ec = lax.broadcasted_iota(jnp.int32, (N_EXPERTS, N_EXPERTS), 1)
    upper = jnp.where(er < ec, 1.0, 0.0).astype(BF16)
    seg8 = jnp.broadcast_to(seg, (8, N_EXPERTS)).astype(BF16)
    seg_start = _dot(seg8, upper)[0:1, :]
    local = before + seg_start
    rows = [jnp.sum(jnp.where(onehots[k], local, 0.0), axis=1, keepdims=True) for k in range(TOP_K)]
    rk_ref[...] = jnp.concatenate(rows, axis=1).astype(jnp.int32)
    cnt_ref[...] = cnt.astype(jnp.int32)


def _merge(x2, ya, hf, hb, xc, p, yc, norm_w, skip, wa, wb, wc, wo, ln_g, ln_b, rw, rb, layer):
    T = x2.shape[0]
    tm = _route_tile(T)
    kern = functools.partial(_merge_kernel, tm=tm)
    row = lambda i: (i, 0)
    cw = lambda i: (layer, 0, 0)
    return pl.pallas_call(
        kern,
        out_shape=(
            jax.ShapeDtypeStruct((T, D_MODEL), F32),
            jax.ShapeDtypeStruct((T, TOP_K), F32),
            jax.ShapeDtypeStruct((T, TOP_K), jnp.int32),
            jax.ShapeDtypeStruct((T // tm, 1, N_EXPERTS), jnp.int32),
        ),
        grid=(T // tm,),
        in_specs=[
            pl.BlockSpec((tm, D_MODEL), row),
            pl.BlockSpec((tm, 512), row),
            pl.BlockSpec((tm, 512), row),
            pl.BlockSpec((tm, 512), row),
            pl.BlockSpec((tm, 512), row),
            pl.BlockSpec((tm, 512), lambda i: (i, MLO_OFF // 512)),
            pl.BlockSpec((tm, 512), row),
            pl.BlockSpec((tm, 3 * D_MODEL), lambda i: (i, 0)),
            pl.BlockSpec((None, 1, 512), cw),
            pl.BlockSpec((None, 1, 512), cw),
            pl.BlockSpec((None, 512, D_MODEL), cw),
            pl.BlockSpec((None, 512, D_MODEL), cw),
            pl.BlockSpec((None, 512, D_MODEL), cw),
            pl.BlockSpec((None, D_MODEL, D_MODEL), cw),
            pl.BlockSpec((None, 1, D_MODEL), cw),
            pl.BlockSpec((None, 1, D_MODEL), cw),
            pl.BlockSpec((None, D_MODEL, N_EXPERTS), cw),
            pl.BlockSpec((None, 1, N_EXPERTS), cw),
        ],
        out_specs=(
            pl.BlockSpec((tm, D_MODEL), row),
            pl.BlockSpec((tm, TOP_K), row),
            pl.BlockSpec((tm, TOP_K), row),
            pl.BlockSpec((None, 1, N_EXPERTS), lambda i: (i, 0, 0)),
        ),
        compiler_params=_cparams(("parallel",)),
        name="merge_router",
    )(x2, ya, hf, hb, xc, p, yc, p, norm_w, skip, wa, wb, wc, wo, ln_g, ln_b, rw, rb)


SEG_ALIGN = 8
TAB_DST, TAB_LEN, TAB_SRC = 0, N_EXPERTS, 2 * N_EXPERTS
TAIL_DST, TAIL_LEN, TAIL_USED = 0, N_EXPERTS, 2 * N_EXPERTS


def _route_tile(T):
    return min(512, T)


def _seg_rows(tm):
    return tm * TOP_K + N_EXPERTS * SEG_ALIGN


def _segment(tab_ref, e):
    n = pl.multiple_of(tab_ref[0, 0, TAB_LEN + e], SEG_ALIGN)
    src = pl.multiple_of(tab_ref[0, 0, TAB_SRC + e], SEG_ALIGN)
    dst = pl.multiple_of(tab_ref[0, 0, TAB_DST + e], SEG_ALIGN)
    return n, src, dst


def _pair_matrix(lp, values, nb):
    col = lax.broadcasted_iota(jnp.int32, (1, nb), 1)
    m = jnp.zeros((lp.shape[0], nb), F32)
    for k in range(TOP_K):
        m = m + jnp.where(lp[:, k:k + 1] == col, values[k], 0.0)
    return m


def _scatter_kernel(prev_ref, tab_ref, tail_ref, lp_ref, x_ref, o_ref, buf_ref, zero_ref, sems, tsem,
                    *, nb, moe_tm, n_tiles):
    i = pl.program_id(0)
    last = pl.num_programs(0) - 1
    slot = i % 2
    sel = _pair_matrix(lp_ref[...], [1.0] * TOP_K, nb).astype(BF16)
    buf_ref[slot] = lax.dot_general(sel, x_ref[...].astype(BF16), (((0,), (0,)), ((), ())),
                                    preferred_element_type=F32)

    def seg_copy(table, s, e):
        n, src, dst = _segment(table, e)
        return n, pltpu.make_async_copy(buf_ref.at[s, pl.ds(src, n), :], o_ref.at[pl.ds(dst, n), :], sems.at[s])

    @pl.when(i > 0)
    def _():
        for e in range(N_EXPERTS):
            n, cp = seg_copy(prev_ref, 1 - slot, e)
            pl.when(n > 0)(cp.wait)

    for e in range(N_EXPERTS):
        n, cp = seg_copy(tab_ref, slot, e)
        pl.when(n > 0)(cp.start)

    @pl.when(i == last)
    def _():
        for e in range(N_EXPERTS):
            n, cp = seg_copy(tab_ref, slot, e)
            pl.when(n > 0)(cp.wait)

        zero_ref[...] = jnp.zeros(zero_ref.shape, F32)

        def tail_copy(e):
            n = pl.multiple_of(tail_ref[0, 0, TAIL_LEN + e], SEG_ALIGN)
            dst = pl.multiple_of(tail_ref[0, 0, TAIL_DST + e], SEG_ALIGN)
            return n, pltpu.make_async_copy(zero_ref.at[pl.ds(0, n), :], o_ref.at[pl.ds(dst, n), :], tsem)

        for e in range(N_EXPERTS):
            n, cp = tail_copy(e)
            pl.when(n > 0)(cp.start)
        for e in range(N_EXPERTS):
            n, cp = tail_copy(e)
            pl.when(n > 0)(cp.wait)

        def tile_copy(j):
            dst = pl.multiple_of(j * moe_tm, moe_tm)
            return pltpu.make_async_copy(zero_ref, o_ref.at[pl.ds(dst, moe_tm), :], tsem)

        def fill(j, carry):
            cp = tile_copy(j)
            cp.start()
            cp.wait()
            return carry

        lax.fori_loop(tail_ref[0, 0, TAIL_USED], n_tiles, fill, 0)


def _scatter_rows(tab, tail, lpos, x1, rows_pad, moe_tm):
    T = x1.shape[0]
    tm = _route_tile(T)
    nb = _seg_rows(tm)
    kern = functools.partial(_scatter_kernel, nb=nb, moe_tm=moe_tm, n_tiles=rows_pad // moe_tm)
    return pl.pallas_call(
        kern,
        out_shape=jax.ShapeDtypeStruct((rows_pad, D_MODEL), F32),
        grid=(T // tm,),
        in_specs=[
            pl.BlockSpec((1, 1, 4 * N_EXPERTS), lambda i: (jnp.maximum(i - 1, 0), 0, 0), memory_space=pltpu.SMEM),
            pl.BlockSpec((1, 1, 4 * N_EXPERTS), lambda i: (i, 0, 0), memory_space=pltpu.SMEM),
            pl.BlockSpec((1, 1, 4 * N_EXPERTS), lambda i: (0, 0, 0), memory_space=pltpu.SMEM),
            pl.BlockSpec((tm, TOP_K), lambda i: (i, 0)),
            pl.BlockSpec((tm, D_MODEL), lambda i: (i, 0)),
        ],
        out_specs=pl.BlockSpec(memory_space=pl.ANY),
        scratch_shapes=[pltpu.VMEM((2, nb, D_MODEL), F32), pltpu.VMEM((moe_tm, D_MODEL), F32),
                        pltpu.SemaphoreType.DMA((2,)), pltpu.SemaphoreType.DMA(())],
        compiler_params=_cparams(("arbitrary",)),
        name="moe_scatter",
    )(tab, tab, tail, lpos, x1)


GU_GROUP = 2 * LANES


def _regroup_kernel(w_ref, o_ref):
    ri = lax.broadcasted_iota(jnp.int32, (GU_GROUP, GU_GROUP), 0)
    ci = lax.broadcasted_iota(jnp.int32, (GU_GROUP, GU_GROUP), 1)
    src = jnp.where(ci < LANES, 2 * ci, 2 * (ci - LANES) + 1)
    perm = jnp.where(ri == src, 1.0, 0.0).astype(BF16)
    for b in range(2 * D_FF // GU_GROUP):
        cols = slice(b * GU_GROUP, (b + 1) * GU_GROUP)
        o_ref[:, cols] = _dot(w_ref[:, cols].astype(BF16), perm).astype(BF16)


def _regroup_gate_up(w_gu):
    L, E = w_gu.shape[0], w_gu.shape[1]
    tr = 512
    return pl.pallas_call(
        _regroup_kernel,
        out_shape=jax.ShapeDtypeStruct((L, E, D_MODEL, 2 * D_FF), BF16),
        grid=(L, E, D_MODEL // tr),
        in_specs=[pl.BlockSpec((None, None, tr, 2 * D_FF), lambda l, e, r: (l, e, r, 0))],
        out_specs=pl.BlockSpec((None, None, tr, 2 * D_FF), lambda l, e, r: (l, e, r, 0)),
        compiler_params=_cparams(("parallel", "parallel", "parallel")),
        name="regroup_gate_up",
    )(w_gu)


def _regroup_bias(b_gu):
    lead = b_gu.shape[:-1]
    g = b_gu.reshape(lead + (2 * D_FF // GU_GROUP, LANES, 2))
    return jnp.swapaxes(g, -1, -2).reshape(lead + (2 * D_FF,))


def _moe_kernel(te_ref, tr_ref, nt_ref, xs_ref, wgu_ref, bgu_ref, wd_ref, bd_ref, o_ref):
    i = pl.program_id(0)

    @pl.when(i < nt_ref[0])
    def _():
        h = _dot(xs_ref[...].astype(BF16), wgu_ref[...]) + bgu_ref[...]
        acts = []
        for b in range(D_FF // LANES):
            gate = jnp.minimum(h[:, 2 * b * LANES:(2 * b + 1) * LANES], SWIGLU_LIMIT)
            up = jnp.clip(h[:, (2 * b + 1) * LANES:(2 * b + 2) * LANES], -SWIGLU_LIMIT, SWIGLU_LIMIT)
            acts.append((gate * jax.nn.sigmoid(SWIGLU_ALPHA * gate) * (up + 1.0)).astype(BF16))
        act = jnp.concatenate(acts, axis=1)
        y = _dot(act, wd_ref[...].astype(BF16)) + bd_ref[...]
        o_ref[...] = y

    @pl.when(i >= nt_ref[0])
    def _():
        o_ref[...] = jnp.zeros(o_ref.shape, F32)


def _moe_experts(tile_expert, tile_row, n_tiles_used, xs, wgu, bgu, wd, bd, layer, tm):
    rows_pad = xs.shape[0]
    n_tiles = rows_pad // tm
    return pl.pallas_call(
        _moe_kernel,
        out_shape=jax.ShapeDtypeStruct((rows_pad, D_MODEL), F32),
        grid_spec=pltpu.PrefetchScalarGridSpec(
            num_scalar_prefetch=3,
            grid=(n_tiles,),
            in_specs=[
                pl.BlockSpec((tm, D_MODEL), lambda i, te, tr, nt: (tr[i], 0)),
                pl.BlockSpec((None, None, D_MODEL, 2 * D_FF), lambda i, te, tr, nt: (layer, te[i], 0, 0)),
                pl.BlockSpec((None, None, 1, 2 * D_FF), lambda i, te, tr, nt: (layer, te[i], 0, 0)),
                pl.BlockSpec((None, None, D_FF, D_MODEL), lambda i, te, tr, nt: (layer, te[i], 0, 0)),
                pl.BlockSpec((None, None, 1, D_MODEL), lambda i, te, tr, nt: (layer, te[i], 0, 0)),
            ],
            out_specs=pl.BlockSpec((tm, D_MODEL), lambda i, te, tr, nt: (i, 0)),
        ),
        compiler_params=_cparams(("arbitrary",)),
        name="moe_experts",
    )(tile_expert, tile_row, n_tiles_used, xs, wgu, bgu, wd, bd)


def _combine_kernel(tab_ref, next_ref, lp_ref, x_ref, tw_ref, lg_ref, lb_ref, ys_ref, o_ref, buf_ref, sems,
                    *, nb):
    i = pl.program_id(0)
    last = pl.num_programs(0) - 1
    slot = i % 2

    def seg_copy(table, s, e):
        n, dst, src = _segment(table, e)
        return n, pltpu.make_async_copy(ys_ref.at[pl.ds(src, n), :], buf_ref.at[s, pl.ds(dst, n), :], sems.at[s])

    def start_all(table, s):
        for e in range(N_EXPERTS):
            n, cp = seg_copy(table, s, e)
            pl.when(n > 0)(cp.start)

    @pl.when(i == 0)
    def _():
        buf_ref[...] = jnp.zeros(buf_ref.shape, F32)
        start_all(tab_ref, slot)

    @pl.when(i < last)
    def _():
        start_all(next_ref, 1 - slot)

    for e in range(N_EXPERTS):
        n, cp = seg_copy(tab_ref, slot, e)
        pl.when(n > 0)(cp.wait)

    tw = tw_ref[...]
    wsel = _pair_matrix(lp_ref[...], [tw[:, k:k + 1] for k in range(TOP_K)], nb)
    y = _dot(wsel.astype(BF16), buf_ref[slot].astype(BF16))
    o_ref[...] = _layer_norm(DN_ALPHA * x_ref[...] + y, lg_ref[...], lb_ref[...])


def _combine(tab, lpos, x1, tw, ln_g, ln_b, ys, layer):
    T = x1.shape[0]
    tm = _route_tile(T)
    nb = _seg_rows(tm)
    nt = T // tm
    kern = functools.partial(_combine_kernel, nb=nb)
    return pl.pallas_call(
        kern,
        out_shape=jax.ShapeDtypeStruct((T, D_MODEL), F32),
        grid=(T // tm,),
        in_specs=[
            pl.BlockSpec((1, 1, 4 * N_EXPERTS), lambda i: (i, 0, 0), memory_space=pltpu.SMEM),
            pl.BlockSpec((1, 1, 4 * N_EXPERTS), lambda i: (jnp.minimum(i + 1, nt - 1), 0, 0),
                         memory_space=pltpu.SMEM),
            pl.BlockSpec((tm, TOP_K), lambda i: (i, 0)),
            pl.BlockSpec((tm, D_MODEL), lambda i: (i, 0)),
            pl.BlockSpec((tm, TOP_K), lambda i: (i, 0)),
            pl.BlockSpec((None, 1, D_MODEL), lambda i: (layer, 0, 0)),
            pl.BlockSpec((None, 1, D_MODEL), lambda i: (layer, 0, 0)),
            pl.BlockSpec(memory_space=pl.ANY),
        ],
        out_specs=pl.BlockSpec((tm, D_MODEL), lambda i: (i, 0)),
        scratch_shapes=[pltpu.VMEM((2, nb, D_MODEL), F32), pltpu.SemaphoreType.DMA((2,))],
        compiler_params=_cparams(("arbitrary",)),
        name="moe_combine",
    )(tab, tab, lpos, x1, tw, ln_g, ln_b, ys)


def _block_diag(w):
    L, H, d, e = w.shape
    eye = jnp.eye(H, dtype=w.dtype)
    return jnp.einsum('lhde,hg->lhdge', w, eye).reshape(L, H * d, H * e)


def _rope_tables(seq):
    pos = jnp.arange(seq, dtype=F32)
    inv = ROPE_THETA ** (-jnp.arange(0, HEAD_DIM, 2, dtype=F32) / HEAD_DIM)
    ang = pos[:, None] * inv[None, :]
    cos = jnp.tile(jnp.cos(ang), (1, 4))
    sin = jnp.tile(jnp.sin(ang), (1, 4))
    sign = jnp.where(jnp.arange(LANES) < 64, -1.0, 1.0).astype(F32)
    return cos, sin * sign[None, :]


def _routing_tables(counts, tm, rows_pad):
    cnt = counts[:, 0, :]
    nt = cnt.shape[0]
    seg = ((cnt + SEG_ALIGN - 1) // SEG_ALIGN) * SEG_ALIGN
    local = jnp.cumsum(seg, axis=1) - seg
    gsize = jnp.sum(seg, axis=0)
    gpad = ((gsize + tm - 1) // tm) * tm
    ends = jnp.cumsum(gpad)
    offs = ends - gpad
    dst = offs[None, :] + jnp.cumsum(seg, axis=0) - seg
    zeros = jnp.zeros_like(seg)
    tab = jnp.concatenate([dst, seg, local, zeros], axis=1).astype(jnp.int32).reshape(nt, 1, 4 * N_EXPERTS)
    n_tiles = rows_pad // tm
    used = (ends[-1] // tm).astype(jnp.int32)
    tail = jnp.concatenate([offs + gsize, gpad - gsize, jnp.broadcast_to(used, (N_EXPERTS,)),
                            jnp.zeros((N_EXPERTS,), jnp.int32)]).astype(jnp.int32).reshape(1, 1, 4 * N_EXPERTS)
    starts = jnp.arange(n_tiles, dtype=jnp.int32) * tm
    te = jnp.sum((starts[:, None] >= ends[None, :]).astype(jnp.int32), axis=1)
    last = jnp.maximum(used - 1, 0)
    te_last = te[last]
    valid = jnp.arange(n_tiles, dtype=jnp.int32) < used
    tile_expert = jnp.where(valid, te, te_last).astype(jnp.int32)
    tile_row = jnp.where(valid, jnp.arange(n_tiles, dtype=jnp.int32), last).astype(jnp.int32)
    return tab, tail, tile_expert, tile_row, used.reshape(1)


def kernel(x, w_in, da_lambda, da_subln, w_proj_a, ml_conv_w, ml_conv_b, ml_wq, ml_wk, ml_wv,
           ml_w_gates, ml_b_gates, ml_norm, ml_skip, w_proj_b, wa_sinks, w_proj_c, w_out,
           ln1_g, ln1_b, router_w, router_b, exp_w_gu, exp_b_gu, exp_w_down, exp_b_down,
           ln2_g, ln2_b):
    B, S, _ = x.shape
    T = B * S
    L = w_in.shape[0]
    moe_tm = min(512, T)
    route_tiles = T // _route_tile(T)
    rows_max = T * TOP_K + route_tiles * N_EXPERTS * SEG_ALIGN + N_EXPERTS * moe_tm
    rows_pad = ((rows_max + moe_tm - 1) // moe_tm) * moe_tm

    col_idx, col_scale = _proj_columns()
    w_all = (w_in[:, :, col_idx] * col_scale[None, None, :]).astype(BF16)
    cos_t, sin_t = _rope_tables(S)
    wqk = jnp.concatenate([_block_diag(ml_wq), _block_diag(ml_wk)], axis=2).astype(BF16)
    wv = _block_diag(ml_wv).astype(BF16)
    wkt = jnp.swapaxes(_block_diag(ml_wk), 1, 2).astype(BF16)
    wg = ml_w_gates.astype(BF16)
    wgt = jnp.swapaxes(ml_w_gates, 1, 2).astype(BF16)
    bg = ml_b_gates[:, None, :]
    bgt = ml_b_gates[:, :, None]
    wgu = _regroup_gate_up(exp_w_gu)
    bgu = _regroup_bias(exp_b_gu)[:, :, None, :]
    bd = exp_b_down[:, :, None, :]
    sub = da_subln[:, None, :]
    wa_b, wb_b, wc_b, wo_b = (w.astype(BF16) for w in (w_proj_a, w_proj_b, w_proj_c, w_out))
    r3 = lambda a: a[:, None, :]

    x2 = x.reshape(T, D_MODEL)
    for l in range(L):
        p = _inproj(x2, w_all, l, cos_t, sin_t, S)
        ya = _diff_attention(p, da_lambda, sub, l, B, S)
        yc = _window_attention(p, r3(wa_sinks), l, B, S)
        q, kt, v, xc, gc, gr = _ml_prep(p, ml_conv_w, r3(ml_conv_b), wqk, wv, wkt, wg, wgt, bg, bgt, l, S)
        hf, hb = _ml_scan(q, kt, v, gc, gr, B, S)
        x1, top_w, lpos, counts = _merge(
            x2, ya, hf, hb, xc, p, yc, r3(ml_norm), r3(ml_skip), wa_b, wb_b, wc_b, wo_b,
            r3(ln1_g), r3(ln1_b), router_w, r3(router_b), l)
        tab, tail, tile_expert, tile_row, used = _routing_tables(counts, moe_tm, rows_pad)
        xs = _scatter_rows(tab, tail, lpos, x1, rows_pad, moe_tm)
        ys = _moe_experts(tile_expert, tile_row, used, xs, wgu, bgu, exp_w_down, bd, l, moe_tm)
        x2 = _combine(tab, lpos, x1, top_w, r3(ln2_g), r3(ln2_b), ys, l)
    return x2.reshape(B, S, D_MODEL)
```

```python
import functools
import math

import jax
import jax.numpy as jnp
import numpy as np
from jax import lax
from jax.experimental import pallas as pl
from jax.experimental.pallas import tpu as pltpu

F32 = jnp.float32
BF16 = jnp.bfloat16

D_MODEL = 1024
DEPTH = 4
DA_HEADS = 4
HEAD_DIM = 64
ML_HEADS = 4
ML_HEAD_DIM = 128
ML_CONV = 5
ML_CHUNK = 128
WA_Q_HEADS = 8
WA_KV_HEADS = 2
WA_BLOCK = 128
N_EXPERTS = 32
TOP_K = 4
D_FF = D_MODEL
SWIGLU_LIMIT = 7.0
SWIGLU_ALPHA = 1.702
ROPE_THETA = 10000.0
DN_ALPHA = (2.0 * DEPTH) ** 0.25
EPS = 1e-5

LANES = 128
VMEM_LIMIT = 56 * 1024 * 1024

GATE_OFF = 0
QA_OFF = 3072
KA_OFF = 3584
QC_OFF = 4096
KCD_OFF = 4608
VCD_OFF = 4864
VA_OFF = 5120
MLX_OFF = 5632
MLO_OFF = 6144
P_COLS = 6656
PROJ_TN = 512
ROPE_FULL_BLOCKS = (QA_OFF // PROJ_TN, KA_OFF // PROJ_TN, QC_OFF // PROJ_TN)
ROPE_HALF_BLOCK = KCD_OFF // PROJ_TN

_R_DA_Q, _R_DA_K, _R_DA_V = 0, 512, 1024
_R_ML_X, _R_ML_O = 1536, 2048
_R_WA_Q, _R_WA_K, _R_WA_V, _R_GATES = 2560, 3072, 3200, 3328


def _pair_perm():
    return np.concatenate([np.arange(0, 32), np.arange(64, 96), np.arange(32, 64), np.arange(96, 128)])


def _proj_columns():
    perm = _pair_perm()
    idx = np.zeros((P_COLS,), np.int32)
    scale = np.ones((P_COLS,), np.float32)
    idx[GATE_OFF:GATE_OFF + 3072] = _R_GATES + np.arange(3072)
    for t in range(4):
        idx[QA_OFF + t * 128:QA_OFF + (t + 1) * 128] = _R_DA_Q + t * 128 + perm
        idx[KA_OFF + t * 128:KA_OFF + (t + 1) * 128] = _R_DA_K + t * 128 + perm
        idx[QC_OFF + t * 128:QC_OFF + (t + 1) * 128] = _R_WA_Q + t * 128 + perm
    scale[QA_OFF:QA_OFF + 512] = HEAD_DIM ** -0.5
    scale[QC_OFF:QC_OFF + 512] = HEAD_DIM ** -0.5
    dup = np.concatenate([np.arange(64), np.arange(64)])
    for g in range(2):
        idx[KCD_OFF + g * 128:KCD_OFF + (g + 1) * 128] = _R_WA_K + g * 64 + dup[perm]
        idx[VCD_OFF + g * 128:VCD_OFF + (g + 1) * 128] = _R_WA_V + g * 64 + dup
    idx[VA_OFF:VA_OFF + 512] = _R_DA_V + np.arange(512)
    idx[MLX_OFF:MLX_OFF + 512] = _R_ML_X + np.arange(512)
    idx[MLO_OFF:MLO_OFF + 512] = _R_ML_O + np.arange(512)
    return idx, scale


def _cparams(sem, vmem=VMEM_LIMIT):
    return pltpu.CompilerParams(dimension_semantics=sem, vmem_limit_bytes=vmem)


def _dot(a, b):
    return jnp.dot(a, b, preferred_element_type=F32)


def _dot_nt(a, b):
    return lax.dot_general(a, b, (((1,), (1,)), ((), ())), preferred_element_type=F32)


def _lane_repeat(x, reps):
    return jnp.concatenate([x] * reps, axis=1)


def _split_bf16(x):
    hi = x.astype(BF16)
    lo = (x - hi.astype(F32)).astype(BF16)
    return hi, lo


def _inproj_kernel(x_ref, w_ref, c_ref, s_ref, o_ref, xb_ref):
    j = pl.program_id(1)

    @pl.when(j == 0)
    def _():
        xb_ref[...] = x_ref[...].astype(BF16)

    def rope(a):
        return a * c_ref[...] + pltpu.roll(a, 64, 1) * s_ref[...]

    is_full = (j == ROPE_FULL_BLOCKS[0]) | (j == ROPE_FULL_BLOCKS[1]) | (j == ROPE_FULL_BLOCKS[2])
    is_half = j == ROPE_HALF_BLOCK

    def roped_store(n_rope):
        acc = _dot(xb_ref[...], w_ref[...])
        for t in range(PROJ_TN // LANES):
            a = acc[:, t * LANES:(t + 1) * LANES]
            o_ref[:, t * LANES:(t + 1) * LANES] = (rope(a) if t < n_rope else a).astype(BF16)

    @pl.when(is_full)
    def _():
        roped_store(4)

    @pl.when(is_half)
    def _():
        roped_store(2)

    @pl.when(jnp.logical_not(is_full | is_half))
    def _():
        o_ref[...] = _dot(xb_ref[...], w_ref[...]).astype(BF16)


def _inproj(x2, w_all, layer, cos_t, sin_t, seq):
    T = x2.shape[0]
    tm = min(2048, seq)
    ns = seq // tm
    return pl.pallas_call(
        _inproj_kernel,
        out_shape=jax.ShapeDtypeStruct((T, P_COLS), BF16),
        grid=(T // tm, P_COLS // PROJ_TN),
        in_specs=[
            pl.BlockSpec((tm, D_MODEL), lambda i, j: (i, 0)),
            pl.BlockSpec((None, D_MODEL, PROJ_TN), lambda i, j: (layer, 0, j)),
            pl.BlockSpec((tm, LANES), lambda i, j: (i % ns, 0)),
            pl.BlockSpec((tm, LANES), lambda i, j: (i % ns, 0)),
        ],
        out_specs=pl.BlockSpec((tm, PROJ_TN), lambda i, j: (i, j)),
        scratch_shapes=[pltpu.VMEM((tm, D_MODEL), BF16)],
        compiler_params=_cparams(("parallel", "arbitrary")),
        name="inproj",
    )(x2, w_all, cos_t, sin_t)


def _da_kernel(lp_ref, q_ref, k_ref, v_ref, sub_ref, o_ref, qs_ref, m_ref, l_ref, acc_ref,
               *, tq, tk, lam_init):
    seq = k_ref.shape[0]
    lane = lax.broadcasted_iota(jnp.int32, (1, LANES), 1)
    map0 = (lane % 64) < 32
    q = q_ref[...].astype(F32)
    qs_ref[0:tq, :] = jnp.where(map0, q, 0.0).astype(BF16)
    qs_ref[tq:2 * tq, :] = jnp.where(map0, 0.0, q).astype(BF16)
    m_ref[...] = jnp.full(m_ref.shape, -jnp.inf, F32)
    l_ref[...] = jnp.zeros(l_ref.shape, F32)
    acc_ref[...] = jnp.zeros(acc_ref.shape, F32)
    reps = tk // LANES

    def body(c, carry):
        start = pl.multiple_of(c * tk, tk)
        s = _dot_nt(qs_ref[...], k_ref[pl.ds(start, tk), :])
        m_old = m_ref[...]
        m_new = jnp.maximum(m_old, jnp.max(s, axis=1, keepdims=True))
        alpha = jnp.exp(m_old - m_new)
        p = jnp.exp(s - _lane_repeat(m_new, reps))
        l_ref[...] = alpha * l_ref[...] + jnp.sum(p, axis=1, keepdims=True)
        acc_ref[...] = alpha * acc_ref[...] + _dot(p.astype(BF16), v_ref[pl.ds(start, tk), :])
        m_ref[...] = m_new
        return carry

    lax.fori_loop(0, seq // tk, body, 0, unroll=4)

    lp = lp_ref[...]
    l01 = jnp.sum(lp[0:1, :] * lp[1:2, :], axis=1, keepdims=True)
    l23 = jnp.sum(lp[2:3, :] * lp[3:4, :], axis=1, keepdims=True)
    lam = jnp.exp(l01) - jnp.exp(l23) + lam_init
    o = acc_ref[...] / l_ref[...]
    a = o[0:tq, :] - lam * o[tq:2 * tq, :]
    ms = jnp.mean(a * a, axis=1, keepdims=True)
    a = a * lax.rsqrt(ms + EPS) * sub_ref[...] * (1.0 - lam_init)
    o_ref[...] = a.astype(BF16)


def _diff_attention(p, lam_params, subln, layer, batch, seq):
    T = p.shape[0]
    tq = min(512, seq)
    tk = min(1024, seq)
    nq = seq // tq
    lam_init = 0.8 - 0.6 * math.exp(-0.3 * layer)
    kern = functools.partial(_da_kernel, tq=tq, tk=tk, lam_init=lam_init)
    return pl.pallas_call(
        kern,
        out_shape=jax.ShapeDtypeStruct((T, DA_HEADS * LANES), BF16),
        grid=(batch, DA_HEADS, nq),
        in_specs=[
            pl.BlockSpec((None, 4, HEAD_DIM), lambda b, h, i: (layer, 0, 0)),
            pl.BlockSpec((tq, LANES), lambda b, h, i: (b * nq + i, QA_OFF // LANES + h)),
            pl.BlockSpec((seq, LANES), lambda b, h, i: (b, KA_OFF // LANES + h)),
            pl.BlockSpec((seq, LANES), lambda b, h, i: (b, VA_OFF // LANES + h)),
            pl.BlockSpec((None, 1, LANES), lambda b, h, i: (layer, 0, 0)),
        ],
        out_specs=pl.BlockSpec((tq, LANES), lambda b, h, i: (b * nq + i, h)),
        scratch_shapes=[
            pltpu.VMEM((2 * tq, LANES), BF16),
            pltpu.VMEM((2 * tq, LANES), F32),
            pltpu.VMEM((2 * tq, LANES), F32),
            pltpu.VMEM((2 * tq, LANES), F32),
        ],
        compiler_params=_cparams(("parallel", "parallel", "arbitrary")),
        name="diff_attn",
    )(lam_params, p, p, p, subln)


def _wa_kernel(sk_ref, q_ref, kp_ref, kc_ref, kn_ref, vp_ref, vc_ref, vn_ref, o_ref, *, nb):
    n = pl.program_id(1)
    blk = WA_BLOCK
    lane = lax.broadcasted_iota(jnp.int32, (1, LANES), 1)
    map0 = (lane % 64) < 32
    qi = lax.broadcasted_iota(jnp.int32, (blk, 3 * blk), 0)
    ki = lax.broadcasted_iota(jnp.int32, (blk, 3 * blk), 1)
    valid = jnp.abs(qi + blk - ki) <= blk
    valid = valid & jnp.logical_not((n == 0) & (ki < blk))
    valid = valid & jnp.logical_not((n == nb - 1) & (ki >= 2 * blk))
    valid4 = jnp.concatenate([valid] * 4, axis=0)
    sk = sk_ref[...]
    for g in range(WA_KV_HEADS):
        gs = slice(g * LANES, (g + 1) * LANES)
        kg = jnp.concatenate([kp_ref[:, gs], kc_ref[:, gs], kn_ref[:, gs]], axis=0)
        vg = jnp.concatenate([vp_ref[:, gs], vc_ref[:, gs], vn_ref[:, gs]], axis=0)
        t0 = q_ref[:, (2 * g) * LANES:(2 * g + 1) * LANES].astype(F32)
        t1 = q_ref[:, (2 * g + 1) * LANES:(2 * g + 2) * LANES].astype(F32)
        qs = jnp.concatenate([jnp.where(map0, t0, 0.0), jnp.where(map0, 0.0, t0),
                              jnp.where(map0, t1, 0.0), jnp.where(map0, 0.0, t1)], axis=0).astype(BF16)
        s = _dot_nt(qs, kg)
        s = jnp.where(valid4, s, -jnp.inf)
        sink = jnp.concatenate(
            [jnp.broadcast_to(sk[:, 4 * g + r:4 * g + r + 1], (blk, LANES)) for r in range(4)], axis=0)
        m = jnp.maximum(jnp.max(s, axis=1, keepdims=True), sink)
        e = jnp.exp(s - _lane_repeat(m, 3))
        inv = 1.0 / (jnp.sum(e, axis=1, keepdims=True) + jnp.exp(sink - m))
        pr = e * _lane_repeat(inv, 3)
        o = _dot(pr.astype(BF16), vg)
        left = lane < 64
        o_ref[:, (2 * g) * LANES:(2 * g + 1) * LANES] = jnp.where(
            left, o[0:blk], o[blk:2 * blk]).astype(BF16)
        o_ref[:, (2 * g + 1) * LANES:(2 * g + 2) * LANES] = jnp.where(
            left, o[2 * blk:3 * blk], o[3 * blk:4 * blk]).astype(BF16)


def _window_attention(p, sinks, layer, batch, seq):
    T = p.shape[0]
    nb = seq // WA_BLOCK
    kcol = KCD_OFF // 256
    vcol = VCD_OFF // 256

    def prev(b, n):
        return b * nb + jnp.maximum(n - 1, 0)

    def nxt(b, n):
        return b * nb + jnp.minimum(n + 1, nb - 1)

    kern = functools.partial(_wa_kernel, nb=nb)
    return pl.pallas_call(
        kern,
        out_shape=jax.ShapeDtypeStruct((T, WA_Q_HEADS * HEAD_DIM), BF16),
        grid=(batch, nb),
        in_specs=[
            pl.BlockSpec((None, 1, WA_Q_HEADS), lambda b, n: (layer, 0, 0)),
            pl.BlockSpec((WA_BLOCK, 512), lambda b, n: (b * nb + n, QC_OFF // 512)),
            pl.BlockSpec((WA_BLOCK, 256), lambda b, n: (prev(b, n), kcol)),
            pl.BlockSpec((WA_BLOCK, 256), lambda b, n: (b * nb + n, kcol)),
            pl.BlockSpec((WA_BLOCK, 256), lambda b, n: (nxt(b, n), kcol)),
            pl.BlockSpec((WA_BLOCK, 256), lambda b, n: (prev(b, n), vcol)),
            pl.BlockSpec((WA_BLOCK, 256), lambda b, n: (b * nb + n, vcol)),
            pl.BlockSpec((WA_BLOCK, 256), lambda b, n: (nxt(b, n), vcol)),
        ],
        out_specs=pl.BlockSpec((WA_BLOCK, 512), lambda b, n: (b * nb + n, 0)),
        compiler_params=_cparams(("parallel", "arbitrary")),
        name="window_attn",
    )(sinks, p, p, p, p, p, p, p)


HALO = 16


def _log_sigmoid(x):
    return jnp.minimum(x, 0.0) - jnp.log(1.0 + jnp.exp(-jnp.abs(x)))


def _mlprep_kernel(xp_ref, xc_ref, xn_ref, cw_ref, cb_ref, wqk_ref, wv_ref, wkt_ref, wg_ref, wgt_ref,
                   bg_ref, bgt_ref, q_ref, kt_ref, v_ref, xco_ref, gc_ref, gr_ref, *, tm, ns):
    i = pl.program_id(0)
    first = (i % ns) == 0
    last = (i % ns) == ns - 1
    xm = xc_ref[...].astype(F32)
    xp = jnp.where(first, 0.0, xp_ref[...].astype(F32))
    xn = jnp.where(last, 0.0, xn_ref[...].astype(F32))
    ext = jnp.concatenate([xp, xm, xn], axis=0)
    cw = cw_ref[...]
    conv = cb_ref[...]
    for j in range(ML_CONV):
        off = HALO + j - ML_CONV // 2
        conv = conv + ext[off:off + tm, :] * cw[j:j + 1, :]
    xc = conv * jax.nn.sigmoid(conv)
    xcb = xc.astype(BF16)
    qk = _dot(xcb, wqk_ref[...])
    v = _dot(xc_ref[...], wv_ref[...])
    kt = _dot_nt(wkt_ref[...], xcb)
    gin = jnp.concatenate([qk, v], axis=1).astype(BF16)
    gcol = _dot(gin, wg_ref[...]) + bg_ref[...]
    grow = _dot_nt(wgt_ref[...], gin) + bgt_ref[...]
    chc = lax.broadcasted_iota(jnp.int32, (1, 16), 1)
    chr_ = lax.broadcasted_iota(jnp.int32, (16, 1), 0)
    gcol = jnp.where((chc // 4) % 2 == 1, _log_sigmoid(gcol), gcol)
    grow = jnp.where((chr_ // 4) % 2 == 1, _log_sigmoid(grow), grow)
    q_ref[...] = qk[:, 0:512].astype(BF16)
    kt_ref[...] = (kt * (ML_HEAD_DIM ** -0.5)).astype(BF16)
    v_ref[...] = v.astype(BF16)
    xco_ref[...] = xcb
    gc_ref[...] = gcol
    gr_ref[...] = grow


def _ml_prep(p, conv_w, conv_b, wqk, wv, wkt, wg, wgt, bg, bgt, layer, seq):
    T = p.shape[0]
    tm = min(512, seq)
    ns = seq // tm
    r = tm // HALO
    nh = T // HALO
    kern = functools.partial(_mlprep_kernel, tm=tm, ns=ns)
    mcol = MLX_OFF // 512
    const2 = lambda i: (layer, 0, 0)
    return pl.pallas_call(
        kern,
        out_shape=(
            jax.ShapeDtypeStruct((T, 512), BF16),
            jax.ShapeDtypeStruct((512, T), BF16),
            jax.ShapeDtypeStruct((T, 512), BF16),
            jax.ShapeDtypeStruct((T, 512), BF16),
            jax.ShapeDtypeStruct((T, 16), F32),
            jax.ShapeDtypeStruct((16, T), F32),
        ),
        grid=(T // tm,),
        in_specs=[
            pl.BlockSpec((HALO, 512), lambda i: (jnp.maximum(i * r - 1, 0), mcol)),
            pl.BlockSpec((tm, 512), lambda i: (i, mcol)),
            pl.BlockSpec((HALO, 512), lambda i: (jnp.minimum((i + 1) * r, nh - 1), mcol)),
            pl.BlockSpec((None, ML_CONV, 512), const2),
            pl.BlockSpec((None, 1, 512), const2),
            pl.BlockSpec((None, 512, 1024), const2),
            pl.BlockSpec((None, 512, 512), const2),
            pl.BlockSpec((None, 512, 512), const2),
            pl.BlockSpec((None, 1536, 16), const2),
            pl.BlockSpec((None, 16, 1536), const2),
            pl.BlockSpec((None, 1, 16), const2),
            pl.BlockSpec((None, 16, 1), const2),
        ],
        out_specs=(
            pl.BlockSpec((tm, 512), lambda i: (i, 0)),
            pl.BlockSpec((512, tm), lambda i: (0, i)),
            pl.BlockSpec((tm, 512), lambda i: (i, 0)),
            pl.BlockSpec((tm, 512), lambda i: (i, 0)),
            pl.BlockSpec((tm, 16), lambda i: (i, 0)),
            pl.BlockSpec((16, tm), lambda i: (0, i)),
        ),
        compiler_params=_cparams(("parallel",)),
        name="mlstm_prep",
    )(p, p, p, conv_w, conv_b, wqk, wv, wkt, wg, wgt, bg, bgt)


def _mlscan_kernel(q_ref, kt_ref, v_ref, gc_ref, gr_ref, o_ref, st_ref, m_ref, *, reverse):
    L = ML_CHUNK

    @pl.when(pl.program_id(1) == 0)
    def _():
        st_ref[...] = jnp.zeros(st_ref.shape, F32)
        m_ref[...] = jnp.zeros(m_ref.shape, F32)

    ri = lax.broadcasted_iota(jnp.int32, (L, L), 0)
    ci = lax.broadcasted_iota(jnp.int32, (L, L), 1)
    if reverse:
        row_cum = (ri >= ci)
        col_cum = (ci >= ri)
        causal = ci >= ri
    else:
        row_cum = (ri <= ci)
        col_cum = (ci <= ri)
        causal = ci <= ri
    row_m = jnp.where(row_cum, 1.0, 0.0).astype(BF16)
    col_m = jnp.where(col_cum, 1.0, 0.0).astype(BF16)

    gr = gr_ref[...]
    gc = gc_ref[...]
    gr_hi, gr_lo = _split_bf16(gr)
    gc_hi, gc_lo = _split_bf16(gc)
    brow = _dot(gr_hi, row_m) + _dot(gr_lo, row_m)
    bcol = _dot(col_m, gc_hi) + _dot(col_m, gc_lo)
    ones_aug = jnp.ones((L, LANES), BF16)
    base = 8 if reverse else 0
    for h in range(ML_HEADS):
        ich, fch = base + h, base + 4 + h
        b_r = brow[fch:fch + 1, :]
        i_r = gr[ich:ich + 1, :]
        b_c = bcol[:, fch:fch + 1]
        g = b_r[:, 0:1] if reverse else b_r[:, L - 1:L]
        w_end = g - b_r + i_r
        m_loc = jnp.max(w_end, axis=1, keepdims=True)
        e_end = jnp.exp(w_end - m_loc)
        m_prev = m_ref[h][:, 0:1]
        m_new = jnp.maximum(g + m_prev, m_loc)
        a = jnp.exp(g + m_prev - m_new)
        bl = jnp.exp(m_loc - m_new)
        hs = slice(h * LANES, (h + 1) * LANES)
        kt_h = kt_ref[hs, :]
        q_h = q_ref[:, hs]
        v_h = v_ref[:, hs]
        aug = jnp.concatenate([v_h, ones_aug], axis=1)
        c_loc = _dot((kt_h.astype(F32) * e_end).astype(BF16), aug)
        st = st_ref[h]
        q_st = _dot(q_h, st.astype(BF16))
        skq = _dot(q_h, kt_h)
        dm = jnp.where(causal, b_c - b_r + i_r, -jnp.inf)
        a_t = b_c + m_prev
        m_t = jnp.maximum(a_t, jnp.max(dm, axis=1, keepdims=True))
        w = jnp.exp(dm - m_t) * skq
        inter = jnp.exp(a_t - m_t)
        numer = inter * q_st[:, 0:LANES] + _dot(w.astype(BF16), v_h)
        denom = inter * q_st[:, LANES:2 * LANES] + jnp.sum(w, axis=1, keepdims=True)
        o_ref[:, hs] = numer / jnp.maximum(jnp.abs(denom), jnp.exp(-m_t))
        st_ref[h] = a * st + bl * c_loc
        m_ref[h] = jnp.broadcast_to(m_new, (1, LANES))


def _ml_scan(q, kt, v, gc, gr, batch, seq, reverse):
    T = q.shape[0]
    nc = seq // ML_CHUNK

    def chunk(b, c):
        cc = nc - 1 - c if reverse else c
        return b * nc + cc

    kern = functools.partial(_mlscan_kernel, reverse=reverse)
    return pl.pallas_call(
        kern,
        out_shape=jax.ShapeDtypeStruct((T, 512), F32),
        grid=(batch, nc),
        in_specs=[
            pl.BlockSpec((ML_CHUNK, 512), lambda b, c: (chunk(b, c), 0)),
            pl.BlockSpec((512, ML_CHUNK), lambda b, c: (0, chunk(b, c))),
            pl.BlockSpec((ML_CHUNK, 512), lambda b, c: (chunk(b, c), 0)),
            pl.BlockSpec((ML_CHUNK, 16), lambda b, c: (chunk(b, c), 0)),
            pl.BlockSpec((16, ML_CHUNK), lambda b, c: (0, chunk(b, c))),
        ],
        out_specs=pl.BlockSpec((ML_CHUNK, 512), lambda b, c: (chunk(b, c), 0)),
        scratch_shapes=[
            pltpu.VMEM((ML_HEADS, ML_HEAD_DIM, 2 * LANES), F32),
            pltpu.VMEM((ML_HEADS, 1, LANES), F32),
        ],
        compiler_params=_cparams(("parallel", "arbitrary")),
        name="mlstm_scan_bwd" if reverse else "mlstm_scan_fwd",
    )(q, kt, v, gc, gr)


def _layer_norm(x, g, b):
    mu = jnp.mean(x, axis=1, keepdims=True)
    xc = x - mu
    var = jnp.mean(xc * xc, axis=1, keepdims=True)
    return xc * lax.rsqrt(var + EPS) * g + b


def _merge_kernel(x_ref, ya_ref, hf_ref, hb_ref, xc_ref, op_ref, yc_ref, g_ref,
                  nw_ref, sk_ref, wa_ref, wb_ref, wc_ref, wo_ref, lg_ref, lb_ref, rw_ref, rb_ref,
                  x1_ref, tw_ref, rk_ref, cnt_ref, *, tm):
    hsum = hf_ref[...] + hb_ref[...]
    parts = []
    for h in range(ML_HEADS):
        hh = hsum[:, h * LANES:(h + 1) * LANES]
        mu = jnp.mean(hh, axis=1, keepdims=True)
        d = hh - mu
        var = jnp.mean(d * d, axis=1, keepdims=True)
        parts.append(d * lax.rsqrt(var + EPS))
    hn = jnp.concatenate(parts, axis=1) * nw_ref[...]
    hn = hn + sk_ref[...] * xc_ref[...].astype(F32)
    yb = (jax.nn.sigmoid(op_ref[...].astype(F32)) * hn).astype(BF16)

    g = g_ref[...].astype(F32)
    merged = (jax.nn.sigmoid(g[:, 0:D_MODEL]) * _dot(ya_ref[...], wa_ref[...])
              + jax.nn.sigmoid(g[:, D_MODEL:2 * D_MODEL]) * _dot(yb, wb_ref[...])
              + jax.nn.sigmoid(g[:, 2 * D_MODEL:3 * D_MODEL]) * _dot(yc_ref[...], wc_ref[...]))
    y = _dot(merged.astype(BF16), wo_ref[...])
    x1 = _layer_norm(DN_ALPHA * x_ref[...] + y, lg_ref[...], lb_ref[...])
    x1_ref[...] = x1

    x_hi, x_lo = _split_bf16(x1)
    rw = rw_ref[...]
    w_hi, w_lo = _split_bf16(rw)
    logits = _dot(x_hi, w_hi) + _dot(x_hi, w_lo) + _dot(x_lo, w_hi) + rb_ref[...]

    eidx = lax.broadcasted_iota(jnp.int32, (tm, N_EXPERTS), 1)
    work = logits
    vals, idxs = [], []
    for _ in range(TOP_K):
        mx = jnp.max(work, axis=1, keepdims=True)
        sel = jnp.min(jnp.where(work == mx, eidx, N_EXPERTS), axis=1, keepdims=True)
        vals.append(mx)
        idxs.append(sel)
        work = jnp.where(eidx == sel, -jnp.inf, work)
    tv = jnp.concatenate(vals, axis=1)
    e = jnp.exp(tv - tv[:, 0:1])
    tw_ref[...] = e / jnp.sum(e, axis=1, keepdims=True)

    onehots = [(eidx == idxs[k]) for k in range(TOP_K)]
    osum = jnp.zeros((tm, N_EXPERTS), F32)
    for k in range(TOP_K):
        osum = osum + jnp.where(onehots[k], 1.0, 0.0)
    ri = lax.broadcasted_iota(jnp.int32, (tm, tm), 0)
    ci = lax.broadcasted_iota(jnp.int32, (tm, tm), 1)
    strict = jnp.where(ci < ri, 1.0, 0.0).astype(BF16)
    before = _dot(strict, osum.astype(BF16))
    cnt = jnp.sum(osum, axis=0, keepdims=True)
    seg = jnp.ceil(cnt * (1.0 / SEG_ALIGN)) * SEG_ALIGN
    er = lax.broadcasted_iota(jnp.int32, (N_EXPERTS, N_EXPERTS), 0)
    ec = lax.broadcasted_iota(jnp.int32, (N_EXPERTS, N_EXPERTS), 1)
    upper = jnp.where(er < ec, 1.0, 0.0).astype(BF16)
    seg8 = jnp.broadcast_to(seg, (8, N_EXPERTS)).astype(BF16)
    seg_start = _dot(seg8, upper)[0:1, :]
    local = before + seg_start
    rows = [jnp.sum(jnp.where(onehots[k], local, 0.0), axis=1, keepdims=True) for k in range(TOP_K)]
    rk_ref[...] = jnp.concatenate(rows, axis=1).astype(jnp.int32)
    cnt_ref[...] = cnt.astype(jnp.int32)


def _merge(x2, ya, hf, hb, xc, p, yc, norm_w, skip, wa, wb, wc, wo, ln_g, ln_b, rw, rb, layer):
    T = x2.shape[0]
    tm = _route_tile(T)
    kern = functools.partial(_merge_kernel, tm=tm)
    row = lambda i: (i, 0)
    cw = lambda i: (layer, 0, 0)
    return pl.pallas_call(
        kern,
        out_shape=(
            jax.ShapeDtypeStruct((T, D_MODEL), F32),
            jax.ShapeDtypeStruct((T, TOP_K), F32),
            jax.ShapeDtypeStruct((T, TOP_K), jnp.int32),
            jax.ShapeDtypeStruct((T // tm, 1, N_EXPERTS), jnp.int32),
        ),
        grid=(T // tm,),
        in_specs=[
            pl.BlockSpec((tm, D_MODEL), row),
            pl.BlockSpec((tm, 512), row),
            pl.BlockSpec((tm, 512), row),
            pl.BlockSpec((tm, 512), row),
            pl.BlockSpec((tm, 512), row),
            pl.BlockSpec((tm, 512), lambda i: (i, MLO_OFF // 512)),
            pl.BlockSpec((tm, 512), row),
            pl.BlockSpec((tm, 3 * D_MODEL), lambda i: (i, 0)),
            pl.BlockSpec((None, 1, 512), cw),
            pl.BlockSpec((None, 1, 512), cw),
            pl.BlockSpec((None, 512, D_MODEL), cw),
            pl.BlockSpec((None, 512, D_MODEL), cw),
            pl.BlockSpec((None, 512, D_MODEL), cw),
            pl.BlockSpec((None, D_MODEL, D_MODEL), cw),
            pl.BlockSpec((None, 1, D_MODEL), cw),
            pl.BlockSpec((None, 1, D_MODEL), cw),
            pl.BlockSpec((None, D_MODEL, N_EXPERTS), cw),
            pl.BlockSpec((None, 1, N_EXPERTS), cw),
        ],
        out_specs=(
            pl.BlockSpec((tm, D_MODEL), row),
            pl.BlockSpec((tm, TOP_K), row),
            pl.BlockSpec((tm, TOP_K), row),
            pl.BlockSpec((None, 1, N_EXPERTS), lambda i: (i, 0, 0)),
        ),
        compiler_params=_cparams(("parallel",)),
        name="merge_router",
    )(x2, ya, hf, hb, xc, p, yc, p, norm_w, skip, wa, wb, wc, wo, ln_g, ln_b, rw, rb)


SEG_ALIGN = 8
TAB_DST, TAB_LEN, TAB_SRC = 0, N_EXPERTS, 2 * N_EXPERTS
TAIL_DST, TAIL_LEN, TAIL_USED = 0, N_EXPERTS, 2 * N_EXPERTS


def _route_tile(T):
    return min(512, T)


def _seg_rows(tm):
    return tm * TOP_K + N_EXPERTS * SEG_ALIGN


def _segment(tab_ref, e):
    n = pl.multiple_of(tab_ref[0, 0, TAB_LEN + e], SEG_ALIGN)
    src = pl.multiple_of(tab_ref[0, 0, TAB_SRC + e], SEG_ALIGN)
    dst = pl.multiple_of(tab_ref[0, 0, TAB_DST + e], SEG_ALIGN)
    return n, src, dst


def _pair_matrix(lp, values, nb):
    col = lax.broadcasted_iota(jnp.int32, (1, nb), 1)
    m = jnp.zeros((lp.shape[0], nb), F32)
    for k in range(TOP_K):
        m = jnp.where(lp[:, k:k + 1] == col, values[k], m)
    return m


def _scatter_kernel(prev_ref, tab_ref, tail_ref, lp_ref, x_ref, o_ref, buf_ref, zero_ref, sems, tsem,
                    *, nb, moe_tm, n_tiles):
    i = pl.program_id(0)
    last = pl.num_programs(0) - 1
    slot = i % 2
    sel = _pair_matrix(lp_ref[...], [1.0] * TOP_K, nb).astype(BF16)
    buf_ref[slot] = lax.dot_general(sel, x_ref[...].astype(BF16), (((0,), (0,)), ((), ())),
                                    preferred_element_type=F32)

    def seg_copy(table, s, e):
        n, src, dst = _segment(table, e)
        return n, pltpu.make_async_copy(buf_ref.at[s, pl.ds(src, n), :], o_ref.at[pl.ds(dst, n), :], sems.at[s])

    @pl.when(i > 0)
    def _():
        for e in range(N_EXPERTS):
            n, cp = seg_copy(prev_ref, 1 - slot, e)
            pl.when(n > 0)(cp.wait)

    for e in range(N_EXPERTS):
        n, cp = seg_copy(tab_ref, slot, e)
        pl.when(n > 0)(cp.start)

    @pl.when(i == last)
    def _():
        for e in range(N_EXPERTS):
            n, cp = seg_copy(tab_ref, slot, e)
            pl.when(n > 0)(cp.wait)

        zero_ref[...] = jnp.zeros(zero_ref.shape, F32)

        def tail_copy(e):
            n = pl.multiple_of(tail_ref[0, 0, TAIL_LEN + e], SEG_ALIGN)
            dst = pl.multiple_of(tail_ref[0, 0, TAIL_DST + e], SEG_ALIGN)
            return n, pltpu.make_async_copy(zero_ref.at[pl.ds(0, n), :], o_ref.at[pl.ds(dst, n), :], tsem)

        for e in range(N_EXPERTS):
            n, cp = tail_copy(e)
            pl.when(n > 0)(cp.start)
        for e in range(N_EXPERTS):
            n, cp = tail_copy(e)
            pl.when(n > 0)(cp.wait)

        def tile_copy(j):
            dst = pl.multiple_of(j * moe_tm, moe_tm)
            return pltpu.make_async_copy(zero_ref, o_ref.at[pl.ds(dst, moe_tm), :], tsem)

        def fill(j, carry):
            cp = tile_copy(j)
            cp.start()
            cp.wait()
            return carry

        lax.fori_loop(tail_ref[0, 0, TAIL_USED], n_tiles, fill, 0)


def _scatter_rows(tab, tail, lpos, x1, rows_pad, moe_tm):
    T = x1.shape[0]
    tm = _route_tile(T)
    nb = _seg_rows(tm)
    kern = functools.partial(_scatter_kernel, nb=nb, moe_tm=moe_tm, n_tiles=rows_pad // moe_tm)
    return pl.pallas_call(
        kern,
        out_shape=jax.ShapeDtypeStruct((rows_pad, D_MODEL), F32),
        grid=(T // tm,),
        in_specs=[
            pl.BlockSpec((1, 1, 4 * N_EXPERTS), lambda i: (jnp.maximum(i - 1, 0), 0, 0), memory_space=pltpu.SMEM),
            pl.BlockSpec((1, 1, 4 * N_EXPERTS), lambda i: (i, 0, 0), memory_space=pltpu.SMEM),
            pl.BlockSpec((1, 1, 4 * N_EXPERTS), lambda i: (0, 0, 0), memory_space=pltpu.SMEM),
            pl.BlockSpec((tm, TOP_K), lambda i: (i, 0)),
            pl.BlockSpec((tm, D_MODEL), lambda i: (i, 0)),
        ],
        out_specs=pl.BlockSpec(memory_space=pl.ANY),
        scratch_shapes=[pltpu.VMEM((2, nb, D_MODEL), F32), pltpu.VMEM((moe_tm, D_MODEL), F32),
                        pltpu.SemaphoreType.DMA((2,)), pltpu.SemaphoreType.DMA(())],
        compiler_params=_cparams(("arbitrary",)),
        name="moe_scatter",
    )(tab, tab, tail, lpos, x1)


GU_GROUP = 2 * LANES


def _regroup_kernel(w_ref, o_ref):
    ri = lax.broadcasted_iota(jnp.int32, (GU_GROUP, GU_GROUP), 0)
    ci = lax.broadcasted_iota(jnp.int32, (GU_GROUP, GU_GROUP), 1)
    src = jnp.where(ci < LANES, 2 * ci, 2 * (ci - LANES) + 1)
    perm = jnp.where(ri == src, 1.0, 0.0).astype(BF16)
    for b in range(2 * D_FF // GU_GROUP):
        cols = slice(b * GU_GROUP, (b + 1) * GU_GROUP)
        o_ref[:, cols] = _dot(w_ref[:, cols].astype(BF16), perm).astype(BF16)


def _regroup_gate_up(w_gu):
    L, E = w_gu.shape[0], w_gu.shape[1]
    tr = 512
    return pl.pallas_call(
        _regroup_kernel,
        out_shape=jax.ShapeDtypeStruct((L, E, D_MODEL, 2 * D_FF), BF16),
        grid=(L, E, D_MODEL // tr),
        in_specs=[pl.BlockSpec((None, None, tr, 2 * D_FF), lambda l, e, r: (l, e, r, 0))],
        out_specs=pl.BlockSpec((None, None, tr, 2 * D_FF), lambda l, e, r: (l, e, r, 0)),
        compiler_params=_cparams(("parallel", "parallel", "parallel")),
        name="regroup_gate_up",
    )(w_gu)


def _regroup_bias(b_gu):
    lead = b_gu.shape[:-1]
    g = b_gu.reshape(lead + (2 * D_FF // GU_GROUP, LANES, 2))
    return jnp.swapaxes(g, -1, -2).reshape(lead + (2 * D_FF,))


def _moe_kernel(te_ref, tr_ref, nt_ref, xs_ref, wgu_ref, bgu_ref, wd_ref, bd_ref, o_ref):
    i = pl.program_id(0)

    @pl.when(i < nt_ref[0])
    def _():
        h = _dot(xs_ref[...].astype(BF16), wgu_ref[...]) + bgu_ref[...]
        acts = []
        for b in range(D_FF // LANES):
            gate = jnp.minimum(h[:, 2 * b * LANES:(2 * b + 1) * LANES], SWIGLU_LIMIT)
            up = jnp.clip(h[:, (2 * b + 1) * LANES:(2 * b + 2) * LANES], -SWIGLU_LIMIT, SWIGLU_LIMIT)
            acts.append((gate * jax.nn.sigmoid(SWIGLU_ALPHA * gate) * (up + 1.0)).astype(BF16))
        act = jnp.concatenate(acts, axis=1)
        y = _dot(act, wd_ref[...].astype(BF16)) + bd_ref[...]
        o_ref[...] = y

    @pl.when(i >= nt_ref[0])
    def _():
        o_ref[...] = jnp.zeros(o_ref.shape, F32)


def _moe_experts(tile_expert, tile_row, n_tiles_used, xs, wgu, bgu, wd, bd, layer, tm):
    rows_pad = xs.shape[0]
    n_tiles = rows_pad // tm
    return pl.pallas_call(
        _moe_kernel,
        out_shape=jax.ShapeDtypeStruct((rows_pad, D_MODEL), F32),
        grid_spec=pltpu.PrefetchScalarGridSpec(
            num_scalar_prefetch=3,
            grid=(n_tiles,),
            in_specs=[
                pl.BlockSpec((tm, D_MODEL), lambda i, te, tr, nt: (tr[i], 0)),
                pl.BlockSpec((None, None, D_MODEL, 2 * D_FF), lambda i, te, tr, nt: (layer, te[i], 0, 0)),
                pl.BlockSpec((None, None, 1, 2 * D_FF), lambda i, te, tr, nt: (layer, te[i], 0, 0)),
                pl.BlockSpec((None, None, D_FF, D_MODEL), lambda i, te, tr, nt: (layer, te[i], 0, 0)),
                pl.BlockSpec((None, None, 1, D_MODEL), lambda i, te, tr, nt: (layer, te[i], 0, 0)),
            ],
            out_specs=pl.BlockSpec((tm, D_MODEL), lambda i, te, tr, nt: (i, 0)),
        ),
        compiler_params=_cparams(("arbitrary",)),
        name="moe_experts",
    )(tile_expert, tile_row, n_tiles_used, xs, wgu, bgu, wd, bd)


def _combine_kernel(tab_ref, next_ref, lp_ref, x_ref, tw_ref, lg_ref, lb_ref, ys_ref, o_ref, buf_ref, sems,
                    *, nb):
    i = pl.program_id(0)
    last = pl.num_programs(0) - 1
    slot = i % 2

    def seg_copy(table, s, e):
        n, dst, src = _segment(table, e)
        return n, pltpu.make_async_copy(ys_ref.at[pl.ds(src, n), :], buf_ref.at[s, pl.ds(dst, n), :], sems.at[s])

    def start_all(table, s):
        for e in range(N_EXPERTS):
            n, cp = seg_copy(table, s, e)
            pl.when(n > 0)(cp.start)

    @pl.when(i == 0)
    def _():
        buf_ref[...] = jnp.zeros(buf_ref.shape, F32)
        start_all(tab_ref, slot)

    @pl.when(i < last)
    def _():
        start_all(next_ref, 1 - slot)

    for e in range(N_EXPERTS):
        n, cp = seg_copy(tab_ref, slot, e)
        pl.when(n > 0)(cp.wait)

    tw = tw_ref[...]
    wsel = _pair_matrix(lp_ref[...], [tw[:, k:k + 1] for k in range(TOP_K)], nb)
    y = _dot(wsel.astype(BF16), buf_ref[slot].astype(BF16))
    o_ref[...] = _layer_norm(DN_ALPHA * x_ref[...] + y, lg_ref[...], lb_ref[...])


def _combine(tab, lpos, x1, tw, ln_g, ln_b, ys, layer):
    T = x1.shape[0]
    tm = _route_tile(T)
    nb = _seg_rows(tm)
    nt = T // tm
    kern = functools.partial(_combine_kernel, nb=nb)
    return pl.pallas_call(
        kern,
        out_shape=jax.ShapeDtypeStruct((T, D_MODEL), F32),
        grid=(T // tm,),
        in_specs=[
            pl.BlockSpec((1, 1, 4 * N_EXPERTS), lambda i: (i, 0, 0), memory_space=pltpu.SMEM),
            pl.BlockSpec((1, 1, 4 * N_EXPERTS), lambda i: (jnp.minimum(i + 1, nt - 1), 0, 0),
                         memory_space=pltpu.SMEM),
            pl.BlockSpec((tm, TOP_K), lambda i: (i, 0)),
            pl.BlockSpec((tm, D_MODEL), lambda i: (i, 0)),
            pl.BlockSpec((tm, TOP_K), lambda i: (i, 0)),
            pl.BlockSpec((None, 1, D_MODEL), lambda i: (layer, 0, 0)),
            pl.BlockSpec((None, 1, D_MODEL), lambda i: (layer, 0, 0)),
            pl.BlockSpec(memory_space=pl.ANY),
        ],
        out_specs=pl.BlockSpec((tm, D_MODEL), lambda i: (i, 0)),
        scratch_shapes=[pltpu.VMEM((2, nb, D_MODEL), F32), pltpu.SemaphoreType.DMA((2,))],
        compiler_params=_cparams(("arbitrary",)),
        name="moe_combine",
    )(tab, tab, lpos, x1, tw, ln_g, ln_b, ys)


def _block_diag(w):
    L, H, d, e = w.shape
    eye = jnp.eye(H, dtype=w.dtype)
    return jnp.einsum('lhde,hg->lhdge', w, eye).reshape(L, H * d, H * e)


def _rope_tables(seq):
    pos = jnp.arange(seq, dtype=F32)
    inv = ROPE_THETA ** (-jnp.arange(0, HEAD_DIM, 2, dtype=F32) / HEAD_DIM)
    ang = pos[:, None] * inv[None, :]
    cos = jnp.tile(jnp.cos(ang), (1, 4))
    sin = jnp.tile(jnp.sin(ang), (1, 4))
    sign = jnp.where(jnp.arange(LANES) < 64, -1.0, 1.0).astype(F32)
    return cos, sin * sign[None, :]


def _routing_tables(counts, tm, rows_pad):
    cnt = counts[:, 0, :]
    nt = cnt.shape[0]
    seg = ((cnt + SEG_ALIGN - 1) // SEG_ALIGN) * SEG_ALIGN
    local = jnp.cumsum(seg, axis=1) - seg
    gsize = jnp.sum(seg, axis=0)
    gpad = ((gsize + tm - 1) // tm) * tm
    ends = jnp.cumsum(gpad)
    offs = ends - gpad
    dst = offs[None, :] + jnp.cumsum(seg, axis=0) - seg
    zeros = jnp.zeros_like(seg)
    tab = jnp.concatenate([dst, seg, local, zeros], axis=1).astype(jnp.int32).reshape(nt, 1, 4 * N_EXPERTS)
    n_tiles = rows_pad // tm
    used = (ends[-1] // tm).astype(jnp.int32)
    tail = jnp.concatenate([offs + gsize, gpad - gsize, jnp.broadcast_to(used, (N_EXPERTS,)),
                            jnp.zeros((N_EXPERTS,), jnp.int32)]).astype(jnp.int32).reshape(1, 1, 4 * N_EXPERTS)
    starts = jnp.arange(n_tiles, dtype=jnp.int32) * tm
    te = jnp.sum((starts[:, None] >= ends[None, :]).astype(jnp.int32), axis=1)
    last = jnp.maximum(used - 1, 0)
    te_last = te[last]
    valid = jnp.arange(n_tiles, dtype=jnp.int32) < used
    tile_expert = jnp.where(valid, te, te_last).astype(jnp.int32)
    tile_row = jnp.where(valid, jnp.arange(n_tiles, dtype=jnp.int32), last).astype(jnp.int32)
    return tab, tail, tile_expert, tile_row, used.reshape(1)


def kernel(x, w_in, da_lambda, da_subln, w_proj_a, ml_conv_w, ml_conv_b, ml_wq, ml_wk, ml_wv,
           ml_w_gates, ml_b_gates, ml_norm, ml_skip, w_proj_b, wa_sinks, w_proj_c, w_out,
           ln1_g, ln1_b, router_w, router_b, exp_w_gu, exp_b_gu, exp_w_down, exp_b_down,
           ln2_g, ln2_b):
    B, S, _ = x.shape
    T = B * S
    L = w_in.shape[0]
    moe_tm = min(512, T)
    route_tiles = T // _route_tile(T)
    rows_max = T * TOP_K + route_tiles * N_EXPERTS * SEG_ALIGN + N_EXPERTS * moe_tm
    rows_pad = ((rows_max + moe_tm - 1) // moe_tm) * moe_tm

    col_idx, col_scale = _proj_columns()
    w_all = (w_in[:, :, col_idx] * col_scale[None, None, :]).astype(BF16)
    cos_t, sin_t = _rope_tables(S)
    wqk = jnp.concatenate([_block_diag(ml_wq), _block_diag(ml_wk)], axis=2).astype(BF16)
    wv = _block_diag(ml_wv).astype(BF16)
    wkt = jnp.swapaxes(_block_diag(ml_wk), 1, 2).astype(BF16)
    wg = ml_w_gates.astype(BF16)
    wgt = jnp.swapaxes(ml_w_gates, 1, 2).astype(BF16)
    bg = ml_b_gates[:, None, :]
    bgt = ml_b_gates[:, :, None]
    wgu = _regroup_gate_up(exp_w_gu)
    bgu = _regroup_bias(exp_b_gu)[:, :, None, :]
    bd = exp_b_down[:, :, None, :]
    sub = da_subln[:, None, :]
    wa_b, wb_b, wc_b, wo_b = (w.astype(BF16) for w in (w_proj_a, w_proj_b, w_proj_c, w_out))
    r3 = lambda a: a[:, None, :]

    x2 = x.reshape(T, D_MODEL)
    for l in range(L):
        p = _inproj(x2, w_all, l, cos_t, sin_t, S)
        ya = _diff_attention(p, da_lambda, sub, l, B, S)
        yc = _window_attention(p, r3(wa_sinks), l, B, S)
        q, kt, v, xc, gc, gr = _ml_prep(p, ml_conv_w, r3(ml_conv_b), wqk, wv, wkt, wg, wgt, bg, bgt, l, S)
        hf = _ml_scan(q, kt, v, gc, gr, B, S, reverse=False)
        hb = _ml_scan(q, kt, v, gc, gr, B, S, reverse=True)
        x1, top_w, lpos, counts = _merge(
            x2, ya, hf, hb, xc, p, yc, r3(ml_norm), r3(ml_skip), wa_b, wb_b, wc_b, wo_b,
            r3(ln1_g), r3(ln1_b), router_w, r3(router_b), l)
        tab, tail, tile_expert, tile_row, used = _routing_tables(counts, moe_tm, rows_pad)
        xs = _scatter_rows(tab, tail, lpos, x1, rows_pad, moe_tm)
        ys = _moe_experts(tile_expert, tile_row, used, xs, wgu, bgu, exp_w_down, bd, l, moe_tm)
        x2 = _combine(tab, lpos, x1, top_w, r3(ln2_g), r3(ln2_b), ys, l)
    return x2.reshape(B, S, D_MODEL)
```
